```python
import math
import jax, jax.numpy as jnp
from jax import lax
import numpy as np

D_MODEL = 1024
BATCH = 32
SEQ = 2048
DEPTH = 1

CHUNK = 64
GMLP_BLOCK = 128
GMLP_GROUPS = 4
GMLP_WIDTH = D_MODEL
GMLP_GROUP_DIM = GMLP_WIDTH // GMLP_GROUPS
SB_HEADS = 16
SB_HEAD_DIM = 64
SB_WIDTH = SB_HEADS * SB_HEAD_DIM
SB_QBLOCK = 128
D_FF = 2816
N_BRANCHES = 2
D_IN = 2 * GMLP_WIDTH + 3 * SB_WIDTH + N_BRANCHES * D_MODEL
EPS = 1e-6

kernel_name = "hybrid_gmlp_stickbreaking_macaron_block"


def rmsnorm(x, g):
    xf = x.astype(jnp.float32)
    y = xf * lax.rsqrt(jnp.mean(xf * xf, axis=-1, keepdims=True) + EPS)
    return (y * g.astype(jnp.float32)).astype(x.dtype)


def layernorm(x, g, b):
    xf = x.astype(jnp.float32)
    mu = jnp.mean(xf, axis=-1, keepdims=True)
    var = jnp.mean(jnp.square(xf - mu), axis=-1, keepdims=True)
    y = (xf - mu) * lax.rsqrt(var + EPS)
    return (y * g.astype(jnp.float32) + b.astype(jnp.float32)).astype(x.dtype)


def swiglu(x, w_gate, w_up, w_down):
    return (jax.nn.silu(x @ w_gate) * (x @ w_up)) @ w_down


def gmlp_spatial_gating(a, ln_g, ln_b, w_s, b_s):
    bsz, seq, _ = a.shape
    nblk = seq // GMLP_BLOCK
    u, v = jnp.split(a, 2, axis=-1)
    u = u.reshape(bsz, nblk, GMLP_BLOCK, GMLP_GROUPS, GMLP_GROUP_DIM)
    v = v.reshape(bsz, nblk, GMLP_BLOCK, GMLP_GROUPS, GMLP_GROUP_DIM)
    v = layernorm(v, ln_g, ln_b)
    pos = jnp.arange(GMLP_BLOCK)
    mask = (pos[None, :] // CHUNK) <= (pos[:, None] // CHUNK)
    w_masked = jnp.where(mask[None], w_s, jnp.zeros((), w_s.dtype))
    s_mix = jnp.einsum('gts,bnsgc->bntgc', w_masked, v) + jnp.transpose(b_s)[None, None, :, :, None]
    return (u * s_mix).reshape(bsz, seq, GMLP_WIDTH)


def stick_breaking_attention(q, k, v):
    q = jnp.transpose(q, (0, 2, 1, 3))
    k = jnp.transpose(k, (0, 2, 1, 3))
    v = jnp.transpose(v, (0, 2, 1, 3))
    seq = q.shape[2]
    scale = 1.0 / math.sqrt(SB_HEAD_DIM)
    outs = []
    for blk in range(seq // SB_QBLOCK):
        q0 = blk * SB_QBLOCK
        kend = q0 + SB_QBLOCK
        qb = q[:, :, q0:kend]
        kb = k[:, :, :kend]
        vb = v[:, :, :kend]
        z = jnp.einsum('bhtd,bhsd->bhts', qb, kb).astype(jnp.float32) * scale
        t_idx = q0 + jnp.arange(SB_QBLOCK)[:, None]
        s_idx = jnp.arange(kend)[None, :]
        causal = s_idx < t_idx
        log_keep = jnp.where(causal, -jax.nn.softplus(z), 0.0)
        log_w = jax.nn.log_sigmoid(z) + lax.cumsum(log_keep, axis=3, reverse=True) - log_keep
        w = jnp.where(causal, jnp.exp(log_w), 0.0)
        outs.append(jnp.einsum('bhts,bhsd->bhtd', w.astype(vb.dtype), vb))
    o = jnp.concatenate(outs, axis=2)
    return jnp.transpose(o, (0, 2, 1, 3))


def setup_inputs(seed: int = 0) -> dict:
    key = jax.random.key(seed)
    ks = jax.random.split(key, 24)
    f32 = jnp.float32

    def nrm(k, shape, fan_in):
        return jax.random.normal(k, shape, f32) * (fan_in ** -0.5)

    def gain(k, shape):
        return 1.0 + 0.05 * jax.random.normal(k, shape, f32)

    L = DEPTH
    return {
        "x": jax.random.normal(ks[0], (BATCH, SEQ, D_MODEL), f32),
        "ff1_norm": gain(ks[1], (L, D_MODEL)),
        "ff1_w_gate": nrm(ks[2], (L, D_MODEL, D_FF), D_MODEL),
        "ff1_w_up": nrm(ks[3], (L, D_MODEL, D_FF), D_MODEL),
        "ff1_w_down": nrm(ks[4], (L, D_FF, D_MODEL), D_FF),
        "mix_norm": gain(ks[5], (L, D_MODEL)),
        "w_in": nrm(ks[6], (L, D_MODEL, D_IN), D_MODEL),
        "b_gate": 0.05 * jax.random.normal(ks[7], (L, N_BRANCHES * D_MODEL), f32),
        "gmlp_ln_g": gain(ks[8], (L, GMLP_GROUPS, GMLP_GROUP_DIM)),
        "gmlp_ln_b": 0.05 * jax.random.normal(ks[9], (L, GMLP_GROUPS, GMLP_GROUP_DIM), f32),
        "gmlp_w_s": nrm(ks[10], (L, GMLP_GROUPS, GMLP_BLOCK, GMLP_BLOCK), GMLP_BLOCK),
        "gmlp_b_s": 1.0 + 0.1 * jax.random.normal(ks[11], (L, GMLP_GROUPS, GMLP_BLOCK), f32),
        "w_branch_a": nrm(ks[12], (L, GMLP_WIDTH, D_MODEL), GMLP_WIDTH),
        "w_branch_b": nrm(ks[13], (L, SB_WIDTH, D_MODEL), SB_WIDTH),
        "w_out": nrm(ks[14], (L, D_MODEL, D_MODEL), D_MODEL),
        "ff2_norm": gain(ks[15], (L, D_MODEL)),
        "ff2_w_gate": nrm(ks[16], (L, D_MODEL, D_FF), D_MODEL),
        "ff2_w_up": nrm(ks[17], (L, D_MODEL, D_FF), D_MODEL),
        "ff2_w_down": nrm(ks[18], (L, D_FF, D_MODEL), D_FF),
        "final_norm": gain(ks[19], (D_MODEL,)),
    }


def reference(x, ff1_norm, ff1_w_gate, ff1_w_up, ff1_w_down, mix_norm, w_in, b_gate,
              gmlp_ln_g, gmlp_ln_b, gmlp_w_s, gmlp_b_s, w_branch_a, w_branch_b, w_out,
              ff2_norm, ff2_w_gate, ff2_w_up, ff2_w_down, final_norm):
    bsz, seq, _ = x.shape
    splits = [2 * GMLP_WIDTH, 2 * GMLP_WIDTH + SB_WIDTH, 2 * GMLP_WIDTH + 2 * SB_WIDTH,
              2 * GMLP_WIDTH + 3 * SB_WIDTH]
    for l in range(DEPTH):
        x = x + 0.5 * swiglu(rmsnorm(x, ff1_norm[l]), ff1_w_gate[l], ff1_w_up[l], ff1_w_down[l])

        h = rmsnorm(x, mix_norm[l])
        proj = h @ w_in[l]
        a_in, q, k, v, g_logits = jnp.split(proj, splits, axis=-1)
        y_a = gmlp_spatial_gating(jax.nn.gelu(a_in, approximate=False),
                                  gmlp_ln_g[l], gmlp_ln_b[l], gmlp_w_s[l], gmlp_b_s[l])
        q = q.reshape(bsz, seq, SB_HEADS, SB_HEAD_DIM)
        k = k.reshape(bsz, seq, SB_HEADS, SB_HEAD_DIM)
        v = v.reshape(bsz, seq, SB_HEADS, SB_HEAD_DIM)
        y_b = stick_breaking_attention(q, k, v).reshape(bsz, seq, SB_WIDTH)
        gates = jax.nn.sigmoid(g_logits + b_gate[l])
        g_a, g_b = jnp.split(gates, 2, axis=-1)
        merged = g_a * (y_a @ w_branch_a[l]) + g_b * (y_b @ w_branch_b[l])
        x = x + merged @ w_out[l]

        x = x + 0.5 * swiglu(rmsnorm(x, ff2_norm[l]), ff2_w_gate[l], ff2_w_up[l], ff2_w_down[l])
    return rmsnorm(x, final_norm)
```

```python
import functools
import math

import jax
import jax.numpy as jnp
from jax import lax
from jax.experimental import pallas as pl
from jax.experimental.pallas import tpu as pltpu

F32 = jnp.float32
BF16 = jnp.bfloat16

EPS = 1e-6
CHUNK = 64
GMLP_BLOCK = 128
GMLP_GROUPS = 4
SB_HEADS = 16
SB_HEAD_DIM = 64
HEADS_PER_SLAB = 2
LANES = 128
FF_CHUNK = 256
ATT_QB = 256
ATT_KC = 256
MIB = 1024 * 1024


def _const_spec(shape):
    nd = len(shape)
    return pl.BlockSpec(shape, lambda *_: (0,) * nd, pipeline_mode=pl.Buffered(1))


def _rms(x, g):
    ms = jnp.mean(x * x, axis=-1, keepdims=True)
    return x * lax.rsqrt(ms + EPS) * g


def _sigmoid(x):
    return 1.0 / (1.0 + jnp.exp(-x))


def _swiglu(h, wg_ref, wu_ref, wd_ref):
    acc = None
    for c in range(wg_ref.shape[0]):
        g = jnp.dot(h, wg_ref[c], preferred_element_type=F32)
        u = jnp.dot(h, wu_ref[c], preferred_element_type=F32)
        a = (g * _sigmoid(g) * u).astype(BF16)
        d = jnp.dot(a, wd_ref[c], preferred_element_type=F32)
        acc = d if acc is None else acc + d
    return acc


def _ff1_kernel(x_ref, n_ref, wg_ref, wu_ref, wd_ref, o_ref):
    x = x_ref[...]
    h = _rms(x, n_ref[...]).astype(BF16)
    o_ref[...] = x + 0.5 * _swiglu(h, wg_ref, wu_ref, wd_ref)


def _ff1(x2d, norm, wg, wu, wd, tm):
    t, d = x2d.shape
    return pl.pallas_call(
        _ff1_kernel,
        out_shape=jax.ShapeDtypeStruct((t, d), F32),
        grid=(t // tm,),
        in_specs=[
            pl.BlockSpec((tm, d), lambda i: (i, 0)),
            _const_spec(norm.shape),
            _const_spec(wg.shape),
            _const_spec(wu.shape),
            _const_spec(wd.shape),
        ],
        out_specs=pl.BlockSpec((tm, d), lambda i: (i, 0)),
        compiler_params=pltpu.CompilerParams(
            dimension_semantics=("arbitrary",), vmem_limit_bytes=52 * MIB),
        name="ff1",
    )(x2d, norm, wg, wu, wd)


def _gelu(x):
    return 0.5 * x * (1.0 + lax.erf(x * (1.0 / math.sqrt(2.0))))


def _proj_kernel(x_ref, n_ref, wu_ref, wv_ref, wq_ref, wkt_ref, wvv_ref, wgt_ref, bg_ref,
                 lng_ref, lnb_ref, ws_ref, bs_ref,
                 ya_ref, q_ref, kt_ref, v_ref, g_ref):
    tm, d = x_ref.shape
    gw = d // GMLP_GROUPS
    h = _rms(x_ref[...], n_ref[...]).astype(BF16)

    u_all = _gelu(jnp.dot(h, wu_ref[...], preferred_element_type=F32))
    v_all = _gelu(jnp.dot(h, wv_ref[...], preferred_element_type=F32))
    pos_t = lax.broadcasted_iota(jnp.int32, (GMLP_BLOCK, GMLP_BLOCK), 0)
    pos_s = lax.broadcasted_iota(jnp.int32, (GMLP_BLOCK, GMLP_BLOCK), 1)
    chunk_mask = (pos_s // CHUNK) <= (pos_t // CHUNK)
    for g in range(GMLP_GROUPS):
        cols = slice(g * gw, (g + 1) * gw)
        vg = v_all[:, cols]
        mu = jnp.mean(vg, axis=-1, keepdims=True)
        dv = vg - mu
        var = jnp.mean(dv * dv, axis=-1, keepdims=True)
        vn = (dv * lax.rsqrt(var + EPS) * lng_ref[:, cols] + lnb_ref[:, cols]).astype(BF16)
        wm = jnp.where(chunk_mask, ws_ref[g], 0.0).astype(BF16)
        for r in range(tm // GMLP_BLOCK):
            rows = slice(r * GMLP_BLOCK, (r + 1) * GMLP_BLOCK)
            s_mix = jnp.dot(wm, vn[rows], preferred_element_type=F32) + bs_ref[g]
            ya_ref[rows, cols] = (u_all[rows, cols] * s_mix).astype(BF16)

    scale = 1.0 / math.sqrt(SB_HEAD_DIM)
    q_ref[...] = (jnp.dot(h, wq_ref[...], preferred_element_type=F32) * scale).astype(BF16)
    kt = lax.dot_general(wkt_ref[...], h, (((1,), (1,)), ((), ())),
                         preferred_element_type=F32).astype(BF16)
    for p in range(kt_ref.shape[1]):
        for c in range(kt_ref.shape[2]):
            kt_ref[0, p, c] = kt[p * LANES:(p + 1) * LANES, c * ATT_KC:(c + 1) * ATT_KC]
    v_ref[...] = jnp.dot(h, wvv_ref[...], preferred_element_type=F32).astype(BF16)

    g_ref[...] = _sigmoid(jnp.dot(h, wgt_ref[...], preferred_element_type=F32) + bg_ref[...])


def _proj(x1, norm, wu, wv, wq, wkt, wvv, wgt, bg, lng, lnb, ws, bs, bsz, seq, tm):
    t, d = x1.shape
    spb = seq // tm
    n_slab = d // LANES
    row = lambda i: (i, 0)
    consts = [norm, wu, wv, wq, wkt, wvv, wgt, bg, lng, lnb, ws, bs]
    return pl.pallas_call(
        _proj_kernel,
        out_shape=(
            jax.ShapeDtypeStruct((t, d), BF16),
            jax.ShapeDtypeStruct((t, d), BF16),
            jax.ShapeDtypeStruct((bsz, n_slab, seq // ATT_KC, LANES, ATT_KC), BF16),
            jax.ShapeDtypeStruct((t, d), BF16),
            jax.ShapeDtypeStruct((t, 2 * d), F32),
        ),
        grid=(t // tm,),
        in_specs=[pl.BlockSpec((tm, d), row)] + [_const_spec(a.shape) for a in consts],
        out_specs=(
            pl.BlockSpec((tm, d), row),
            pl.BlockSpec((tm, d), row),
            pl.BlockSpec((1, n_slab, tm // ATT_KC, LANES, ATT_KC),
                         lambda i: (i // spb, 0, i % spb, 0, 0)),
            pl.BlockSpec((tm, d), row),
            pl.BlockSpec((tm, 2 * d), row),
        ),
        compiler_params=pltpu.CompilerParams(
            dimension_semantics=("arbitrary",), vmem_limit_bytes=52 * MIB),
        name="proj",
    )(x1, *consts)


def _attn_kernel(q_ref, kt_ref, v_ref, o_ref):
    seq = q_ref.shape[1]
    row = lax.broadcasted_iota(jnp.int32, (ATT_QB, ATT_KC), 0)
    col = lax.broadcasted_iota(jnp.int32, (ATT_QB, ATT_KC), 1)
    causal = col < row
    tri = jnp.where(row >= col, 1.0, 0.0).astype(BF16)
    lane = lax.broadcasted_iota(jnp.int32, (ATT_QB, LANES), 1)
    first_head = lane < SB_HEAD_DIM

    def weights(z, c, masked):
        sp = jnp.maximum(z, 0.0) + jnp.log(1.0 + jnp.exp(-jnp.abs(z)))
        if masked:
            sp = jnp.where(causal, sp, 0.0)
        hi = sp.astype(BF16)
        lo = (sp - hi.astype(F32)).astype(BF16)
        rsum = (jnp.dot(hi, tri, preferred_element_type=F32)
                + jnp.dot(lo, tri, preferred_element_type=F32))
        w = jnp.exp(z - rsum - c)
        if masked:
            w = jnp.where(causal, w, 0.0)
        return w.astype(BF16), c + jnp.sum(sp, axis=1, keepdims=True)

    def q_block(m, _):
        r0 = pl.multiple_of(m * ATT_QB, ATT_QB)
        q = q_ref[0, pl.ds(r0, ATT_QB), :]
        zero = jnp.zeros_like(q)
        q2 = jnp.concatenate([jnp.where(first_head, q, zero), jnp.where(first_head, zero, q)], axis=0)

        def step(j, carry, masked):
            c0, c1, acc = carry
            k0 = pl.multiple_of(j * ATT_KC, ATT_KC)
            vc = v_ref[0, pl.ds(k0, ATT_KC), :]
            zero_v = jnp.zeros_like(vc)
            first_head_v = lax.broadcasted_iota(jnp.int32, vc.shape, 1) < SB_HEAD_DIM
            z2 = jnp.dot(q2, kt_ref[0, 0, j], preferred_element_type=F32)
            w0, c0 = weights(z2[:ATT_QB], c0, masked)
            w1, c1 = weights(z2[ATT_QB:], c1, masked)
            acc = (acc
                   + jnp.dot(w0, jnp.where(first_head_v, vc, zero_v), preferred_element_type=F32)
                   + jnp.dot(w1, jnp.where(first_head_v, zero_v, vc), preferred_element_type=F32))
            return c0, c1, acc

        init = (jnp.zeros((ATT_QB, 1), F32), jnp.zeros((ATT_QB, 1), F32),
                jnp.zeros((ATT_QB, LANES), F32))
        carry = step(m, init, True)
        carry = lax.fori_loop(0, m, lambda i, cr: step(m - 1 - i, cr, False), carry)
        o_ref[0, pl.ds(r0, ATT_QB), :] = carry[2].astype(BF16)
        return 0

    lax.fori_loop(0, seq // ATT_QB, q_block, 0)


def _attn(q, kt, v):
    bsz, seq, d = q.shape
    n_slab = d // LANES
    slab = lambda b, p: (b, 0, p)
    return pl.pallas_call(
        _attn_kernel,
        out_shape=jax.ShapeDtypeStruct((bsz, seq, d), BF16),
        grid=(bsz, n_slab),
        in_specs=[
            pl.BlockSpec((1, seq, LANES), slab),
            pl.BlockSpec((1, 1, seq // ATT_KC, LANES, ATT_KC), lambda b, p: (b, p, 0, 0, 0)),
            pl.BlockSpec((1, seq, LANES), slab),
        ],
        out_specs=pl.BlockSpec((1, seq, LANES), slab),
        compiler_params=pltpu.CompilerParams(
            dimension_semantics=("arbitrary", "arbitrary"), vmem_limit_bytes=32 * MIB),
        name="attn",
    )(q, kt, v)


def _merge_kernel(x_ref, ya_ref, yb_ref, g_ref, wa_ref, wb_ref, wo_ref, n2_ref,
                  wg_ref, wu_ref, wd_ref, nf_ref, o_ref):
    d = x_ref.shape[1]
    ma = jnp.dot(ya_ref[...], wa_ref[...], preferred_element_type=F32)
    mb = jnp.dot(yb_ref[...], wb_ref[...], preferred_element_type=F32)
    merged = (g_ref[:, :d] * ma + g_ref[:, d:] * mb).astype(BF16)
    x2 = x_ref[...] + jnp.dot(merged, wo_ref[...], preferred_element_type=F32)
    h = _rms(x2, n2_ref[...]).astype(BF16)
    x3 = x2 + 0.5 * _swiglu(h, wg_ref, wu_ref, wd_ref)
    o_ref[...] = _rms(x3, nf_ref[...])


def _merge(x1, ya, yb, gates, wa, wb, wo, n2, wg, wu, wd, nf, tm):
    t, d = x1.shape
    row = lambda i: (i, 0)
    consts = [wa, wb, wo, n2, wg, wu, wd, nf]
    return pl.pallas_call(
        _merge_kernel,
        out_shape=jax.ShapeDtypeStruct((t, d), F32),
        grid=(t // tm,),
        in_specs=[pl.BlockSpec((tm, d), row), pl.BlockSpec((tm, d), row),
                  pl.BlockSpec((tm, d), row), pl.BlockSpec((tm, 2 * d), row)]
                 + [_const_spec(a.shape) for a in consts],
        out_specs=pl.BlockSpec((tm, d), row),
        compiler_params=pltpu.CompilerParams(
            dimension_semantics=("arbitrary",), vmem_limit_bytes=52 * MIB),
        name="merge_ff2",
    )(x1, ya, yb, gates, *consts)


def _chunk_cols(w):
    d, f = w.shape
    return jnp.transpose(w.reshape(d, f // FF_CHUNK, FF_CHUNK), (1, 0, 2)).astype(BF16)


def _chunk_rows(w):
    f, d = w.shape
    return w.reshape(f // FF_CHUNK, FF_CHUNK, d).astype(BF16)


def _row_tile(seq, want):
    tm = min(want, seq)
    assert seq % tm == 0 and tm % ATT_KC == 0
    return tm


def kernel(x, ff1_norm, ff1_w_gate, ff1_w_up, ff1_w_down, mix_norm, w_in, b_gate, gmlp_ln_g, gmlp_ln_b, gmlp_w_s, gmlp_b_s, w_branch_a, w_branch_b, w_out, ff2_norm, ff2_w_gate, ff2_w_up, ff2_w_down, final_norm):
    bsz, seq, d = x.shape
    assert ff1_norm.shape[0] == 1, "the final RMSNorm is fused into the single layer's last kernel"
    assert seq % ATT_QB == 0 and d == SB_HEADS * SB_HEAD_DIM and ATT_QB == ATT_KC
    gm = d
    x2d = x.reshape(bsz * seq, d)
    x1 = _ff1(x2d, ff1_norm, _chunk_cols(ff1_w_gate[0]), _chunk_cols(ff1_w_up[0]),
              _chunk_rows(ff1_w_down[0]), _row_tile(seq, 512))

    w = w_in[0]
    o_q, o_k, o_v, o_g = 2 * gm, 2 * gm + d, 2 * gm + 2 * d, 2 * gm + 3 * d
    ya, q, kt, v, gates = _proj(
        x1, mix_norm,
        w[:, :gm].astype(BF16), w[:, gm:o_q].astype(BF16), w[:, o_q:o_k].astype(BF16),
        w[:, o_k:o_v].T.astype(BF16), w[:, o_v:o_g].astype(BF16), w[:, o_g:].astype(BF16),
        b_gate, gmlp_ln_g.reshape(1, gm), gmlp_ln_b.reshape(1, gm),
        gmlp_w_s[0], gmlp_b_s[0][:, :, None], bsz, seq, _row_tile(seq, 256))

    yb = _attn(q.reshape(bsz, seq, d), kt, v.reshape(bsz, seq, d)).reshape(bsz * seq, d)

    y = _merge(x1, ya, yb, gates, w_branch_a[0].astype(BF16), w_branch_b[0].astype(BF16),
               w_out[0].astype(BF16), ff2_norm, _chunk_cols(ff2_w_gate[0]),
               _chunk_cols(ff2_w_up[0]), _chunk_rows(ff2_w_down[0]), final_norm[None],
               _row_tile(seq, 256))
    return y.reshape(bsz, seq, d)
```

```python
import math

import jax
import jax.numpy as jnp
from jax import lax
from jax.experimental import pallas as pl
from jax.experimental.pallas import tpu as pltpu

F32 = jnp.float32
BF16 = jnp.bfloat16

EPS = 1e-6
CHUNK = 64
GMLP_BLOCK = 128
GMLP_GROUPS = 4
SB_HEADS = 16
SB_HEAD_DIM = 64
HEADS_PER_SLAB = 2
LANES = 128
FF_CHUNK = 256
ATT_QB = 256
ATT_KC = 256
MIB = 1024 * 1024
LOG2E = 1.4426950408889634
MASKED_LOGIT = -1e30


def _const_spec(shape):
    nd = len(shape)
    return pl.BlockSpec(shape, lambda *_: (0,) * nd, pipeline_mode=pl.Buffered(1))


def _rms(x, g):
    ms = jnp.mean(x * x, axis=-1, keepdims=True)
    return x * lax.rsqrt(ms + EPS) * g


def _sigmoid(x):
    return 1.0 / (1.0 + jnp.exp(-x))


def _swiglu(h, wg_ref, wu_ref, wd_ref):
    acc = None
    for c in range(wg_ref.shape[0]):
        g = jnp.dot(h, wg_ref[c], preferred_element_type=F32)
        u = jnp.dot(h, wu_ref[c], preferred_element_type=F32)
        a = (g * _sigmoid(g) * u).astype(BF16)
        d = jnp.dot(a, wd_ref[c], preferred_element_type=F32)
        acc = d if acc is None else acc + d
    return acc


def _ff1_kernel(x_ref, n_ref, wg_ref, wu_ref, wd_ref, o_ref):
    x = x_ref[...]
    h = _rms(x, n_ref[...]).astype(BF16)
    o_ref[...] = x + 0.5 * _swiglu(h, wg_ref, wu_ref, wd_ref)


def _ff1(x2d, norm, wg, wu, wd, tm):
    t, d = x2d.shape
    return pl.pallas_call(
        _ff1_kernel,
        out_shape=jax.ShapeDtypeStruct((t, d), F32),
        grid=(t // tm,),
        in_specs=[
            pl.BlockSpec((tm, d), lambda i: (i, 0)),
            _const_spec(norm.shape),
            _const_spec(wg.shape),
            _const_spec(wu.shape),
            _const_spec(wd.shape),
        ],
        out_specs=pl.BlockSpec((tm, d), lambda i: (i, 0)),
        compiler_params=pltpu.CompilerParams(
            dimension_semantics=("arbitrary",), vmem_limit_bytes=52 * MIB),
        name="ff1",
    )(x2d, norm, wg, wu, wd)


def _gelu(x):
    return 0.5 * x * (1.0 + lax.erf(x * (1.0 / math.sqrt(2.0))))


def _proj_kernel(x_ref, n_ref, wu_ref, wv_ref, wq_ref, wkt_ref, wvv_ref, wgt_ref, bg_ref,
                 lng_ref, lnb_ref, ws_ref, bs_ref,
                 ya_ref, q_ref, kt_ref, v_ref, g_ref):
    tm, d = x_ref.shape
    gw = d // GMLP_GROUPS
    h = _rms(x_ref[...], n_ref[...]).astype(BF16)

    u_all = _gelu(jnp.dot(h, wu_ref[...], preferred_element_type=F32))
    v_all = _gelu(jnp.dot(h, wv_ref[...], preferred_element_type=F32))
    pos_t = lax.broadcasted_iota(jnp.int32, (GMLP_BLOCK, GMLP_BLOCK), 0)
    pos_s = lax.broadcasted_iota(jnp.int32, (GMLP_BLOCK, GMLP_BLOCK), 1)
    chunk_mask = (pos_s // CHUNK) <= (pos_t // CHUNK)
    for g in range(GMLP_GROUPS):
        cols = slice(g * gw, (g + 1) * gw)
        vg = v_all[:, cols]
        mu = jnp.mean(vg, axis=-1, keepdims=True)
        dv = vg - mu
        var = jnp.mean(dv * dv, axis=-1, keepdims=True)
        vn = (dv * lax.rsqrt(var + EPS) * lng_ref[:, cols] + lnb_ref[:, cols]).astype(BF16)
        wm = jnp.where(chunk_mask, ws_ref[g], 0.0).astype(BF16)
        for r in range(tm // GMLP_BLOCK):
            rows = slice(r * GMLP_BLOCK, (r + 1) * GMLP_BLOCK)
            s_mix = jnp.dot(wm, vn[rows], preferred_element_type=F32) + bs_ref[g]
            ya_ref[rows, cols] = (u_all[rows, cols] * s_mix).astype(BF16)

    scale = LOG2E / math.sqrt(SB_HEAD_DIM)
    q_ref[...] = (jnp.dot(h, wq_ref[...], preferred_element_type=F32) * scale).astype(BF16)
    kt = lax.dot_general(wkt_ref[...], h, (((1,), (1,)), ((), ())),
                         preferred_element_type=F32).astype(BF16)
    for p in range(kt_ref.shape[1]):
        for c in range(kt_ref.shape[2]):
            kt_ref[0, p, c] = kt[p * LANES:(p + 1) * LANES, c * ATT_KC:(c + 1) * ATT_KC]
    v_ref[...] = jnp.dot(h, wvv_ref[...], preferred_element_type=F32).astype(BF16)

    g_ref[...] = _sigmoid(jnp.dot(h, wgt_ref[...], preferred_element_type=F32) + bg_ref[...])


def _proj(x1, norm, wu, wv, wq, wkt, wvv, wgt, bg, lng, lnb, ws, bs, bsz, seq, tm):
    t, d = x1.shape
    spb = seq // tm
    n_slab = d // LANES
    row = lambda i: (i, 0)
    consts = [norm, wu, wv, wq, wkt, wvv, wgt, bg, lng, lnb, ws, bs]
    return pl.pallas_call(
        _proj_kernel,
        out_shape=(
            jax.ShapeDtypeStruct((t, d), BF16),
            jax.ShapeDtypeStruct((t, d), BF16),
            jax.ShapeDtypeStruct((bsz, n_slab, seq // ATT_KC, LANES, ATT_KC), BF16),
            jax.ShapeDtypeStruct((t, d), BF16),
            jax.ShapeDtypeStruct((t, 2 * d), F32),
        ),
        grid=(t // tm,),
        in_specs=[pl.BlockSpec((tm, d), row)] + [_const_spec(a.shape) for a in consts],
        out_specs=(
            pl.BlockSpec((tm, d), row),
            pl.BlockSpec((tm, d), row),
            pl.BlockSpec((1, n_slab, tm // ATT_KC, LANES, ATT_KC),
                         lambda i: (i // spb, 0, i % spb, 0, 0)),
            pl.BlockSpec((tm, d), row),
            pl.BlockSpec((tm, 2 * d), row),
        ),
        compiler_params=pltpu.CompilerParams(
            dimension_semantics=("arbitrary",), vmem_limit_bytes=52 * MIB),
        name="proj",
    )(x1, *consts)


def _attn_kernel(q_ref, kt_ref, v_ref, o_ref, v2_ref):
    seq = q_ref.shape[1]
    n_chunks = seq // ATT_KC
    row = lax.broadcasted_iota(jnp.int32, (ATT_QB, ATT_KC), 0)
    col = lax.broadcasted_iota(jnp.int32, (ATT_QB, ATT_KC), 1)
    causal = col < row
    tri = jnp.where(row > col, 1.0, 0.0).astype(BF16)
    first_head = lax.broadcasted_iota(jnp.int32, (ATT_QB, LANES), 1) < SB_HEAD_DIM

    for j in range(n_chunks):
        vc = v_ref[0, j * ATT_KC:(j + 1) * ATT_KC, :]
        zero_v = jnp.zeros_like(vc)
        v2_ref[j, :ATT_KC, :] = jnp.where(first_head, vc, zero_v)
        v2_ref[j, ATT_KC:, :] = jnp.where(first_head, zero_v, vc)

    def weights(z, c, masked):
        if masked:
            z = jnp.where(causal, z, MASKED_LOGIT)
        neg_abs = lax.bitcast_convert_type(
            lax.bitcast_convert_type(z, jnp.uint32) | jnp.uint32(0x80000000), F32)
        sp = jnp.maximum(z, 0.0) + jnp.log(1.0 + jnp.exp2(neg_abs)) * LOG2E
        later = jnp.dot(sp.astype(BF16), tri, preferred_element_type=F32)
        w = jnp.exp2(z - sp - later - c)
        return w.astype(BF16), c + jnp.sum(sp, axis=1, keepdims=True)

    for m in range(seq // ATT_QB):
        q = q_ref[0, m * ATT_QB:(m + 1) * ATT_QB, :]
        zero = jnp.zeros_like(q)
        q2 = jnp.concatenate([jnp.where(first_head, q, zero), jnp.where(first_head, zero, q)], axis=0)
        c0 = jnp.zeros((ATT_QB, 1), F32)
        c1 = jnp.zeros((ATT_QB, 1), F32)
        ws = []
        for j in range(m, -1, -1):
            z2 = jnp.dot(q2, kt_ref[0, 0, j], preferred_element_type=F32)
            w0, c0 = weights(z2[:ATT_QB], c0, j == m)
            w1, c1 = weights(z2[ATT_QB:], c1, j == m)
            ws += [w0, w1]
        v2 = jnp.concatenate([v2_ref[j] for j in range(m, -1, -1)], axis=0)
        o_ref[0, m * ATT_QB:(m + 1) * ATT_QB, :] = jnp.dot(
            jnp.concatenate(ws, axis=1), v2, preferred_element_type=F32).astype(BF16)


def _attn(q, kt, v):
    bsz, seq, d = q.shape
    n_slab = d // LANES
    slab = lambda b, p: (b, 0, p)
    return pl.pallas_call(
        _attn_kernel,
        out_shape=jax.ShapeDtypeStruct((bsz, seq, d), BF16),
        grid=(bsz, n_slab),
        in_specs=[
            pl.BlockSpec((1, seq, LANES), slab),
            pl.BlockSpec((1, 1, seq // ATT_KC, LANES, ATT_KC), lambda b, p: (b, p, 0, 0, 0)),
            pl.BlockSpec((1, seq, LANES), slab),
        ],
        out_specs=pl.BlockSpec((1, seq, LANES), slab),
        scratch_shapes=[pltpu.VMEM((seq // ATT_KC, HEADS_PER_SLAB * ATT_KC, LANES), BF16)],
        compiler_params=pltpu.CompilerParams(
            dimension_semantics=("arbitrary", "arbitrary"), vmem_limit_bytes=32 * MIB),
        name="attn",
    )(q, kt, v)


def _merge_kernel(x_ref, ya_ref, yb_ref, g_ref, wa_ref, wb_ref, wo_ref, o_ref):
    d = x_ref.shape[1]
    ma = jnp.dot(ya_ref[...], wa_ref[...], preferred_element_type=F32)
    mb = jnp.dot(yb_ref[...], wb_ref[...], preferred_element_type=F32)
    merged = (g_ref[:, :d] * ma + g_ref[:, d:] * mb).astype(BF16)
    o_ref[...] = x_ref[...] + jnp.dot(merged, wo_ref[...], preferred_element_type=F32)


def _merge(x1, ya, yb, gates, wa, wb, wo, tm):
    t, d = x1.shape
    row = lambda i: (i, 0)
    consts = [wa, wb, wo]
    return pl.pallas_call(
        _merge_kernel,
        out_shape=jax.ShapeDtypeStruct((t, d), F32),
        grid=(t // tm,),
        in_specs=[pl.BlockSpec((tm, d), row), pl.BlockSpec((tm, d), row),
                  pl.BlockSpec((tm, d), row), pl.BlockSpec((tm, 2 * d), row)]
                 + [_const_spec(a.shape) for a in consts],
        out_specs=pl.BlockSpec((tm, d), row),
        compiler_params=pltpu.CompilerParams(
            dimension_semantics=("arbitrary",), vmem_limit_bytes=52 * MIB),
        name="merge",
    )(x1, ya, yb, gates, *consts)


def _ff2_kernel(x_ref, n_ref, wg_ref, wu_ref, wd_ref, nf_ref, o_ref):
    x = x_ref[...]
    h = _rms(x, n_ref[...]).astype(BF16)
    o_ref[...] = _rms(x + 0.5 * _swiglu(h, wg_ref, wu_ref, wd_ref), nf_ref[...])


def _ff2(x2d, norm, wg, wu, wd, nf, tm):
    t, d = x2d.shape
    consts = [norm, wg, wu, wd, nf]
    return pl.pallas_call(
        _ff2_kernel,
        out_shape=jax.ShapeDtypeStruct((t, d), F32),
        grid=(t // tm,),
        in_specs=[pl.BlockSpec((tm, d), lambda i: (i, 0))] + [_const_spec(a.shape) for a in consts],
        out_specs=pl.BlockSpec((tm, d), lambda i: (i, 0)),
        compiler_params=pltpu.CompilerParams(
            dimension_semantics=("arbitrary",), vmem_limit_bytes=52 * MIB),
        name="ff2",
    )(x2d, *consts)


def _chunk_cols(w):
    d, f = w.shape
    return jnp.transpose(w.reshape(d, f // FF_CHUNK, FF_CHUNK), (1, 0, 2)).astype(BF16)


def _chunk_rows(w):
    f, d = w.shape
    return w.reshape(f // FF_CHUNK, FF_CHUNK, d).astype(BF16)


def _row_tile(seq, want):
    tm = min(want, seq)
    assert seq % tm == 0 and tm % ATT_KC == 0
    return tm


def kernel(x, ff1_norm, ff1_w_gate, ff1_w_up, ff1_w_down, mix_norm, w_in, b_gate, gmlp_ln_g, gmlp_ln_b, gmlp_w_s, gmlp_b_s, w_branch_a, w_branch_b, w_out, ff2_norm, ff2_w_gate, ff2_w_up, ff2_w_down, final_norm):
    bsz, seq, d = x.shape
    assert ff1_norm.shape[0] == 1, "the final RMSNorm is fused into the single layer's last kernel"
    assert seq % ATT_QB == 0 and d == SB_HEADS * SB_HEAD_DIM and ATT_QB == ATT_KC
    gm = d
    x2d = x.reshape(bsz * seq, d)
    x1 = _ff1(x2d, ff1_norm, _chunk_cols(ff1_w_gate[0]), _chunk_cols(ff1_w_up[0]),
              _chunk_rows(ff1_w_down[0]), _row_tile(seq, 512))

    w = w_in[0]
    o_q, o_k, o_v, o_g = 2 * gm, 2 * gm + d, 2 * gm + 2 * d, 2 * gm + 3 * d
    ya, q, kt, v, gates = _proj(
        x1, mix_norm,
        w[:, :gm].astype(BF16), w[:, gm:o_q].astype(BF16), w[:, o_q:o_k].astype(BF16),
        w[:, o_k:o_v].T.astype(BF16), w[:, o_v:o_g].astype(BF16), w[:, o_g:].astype(BF16),
        b_gate, gmlp_ln_g.reshape(1, gm), gmlp_ln_b.reshape(1, gm),
        gmlp_w_s[0], gmlp_b_s[0][:, :, None], bsz, seq, _row_tile(seq, 512))

    yb = _attn(q.reshape(bsz, seq, d), kt, v.reshape(bsz, seq, d)).reshape(bsz * seq, d)

    x2 = _merge(x1, ya, yb, gates, w_branch_a[0].astype(BF16), w_branch_b[0].astype(BF16),
                w_out[0].astype(BF16), _row_tile(seq, 512))
    y = _ff2(x2, ff2_norm, _chunk_cols(ff2_w_gate[0]), _chunk_cols(ff2_w_up[0]),
             _chunk_rows(ff2_w_down[0]), final_norm[None], _row_tile(seq, 512))
    return y.reshape(bsz, seq, d)
```

```python
import math

import jax
import jax.numpy as jnp
from jax import lax
from jax.experimental import pallas as pl
from jax.experimental.pallas import tpu as pltpu

F32 = jnp.float32
BF16 = jnp.bfloat16

EPS = 1e-6
CHUNK = 64
GMLP_BLOCK = 128
GMLP_GROUPS = 4
SB_HEADS = 16
SB_HEAD_DIM = 64
HEADS_PER_SLAB = 2
LANES = 128
FF_CHUNK = 256
ATT_QB = 256
ATT_KC = 256
MIB = 1024 * 1024
LOG2E = 1.4426950408889634
MASKED_LOGIT = -1e30
ATT_SKIP_LOG2 = 160.0


def _const_spec(shape):
    nd = len(shape)
    return pl.BlockSpec(shape, lambda *_: (0,) * nd, pipeline_mode=pl.Buffered(1))


def _rms(x, g):
    ms = jnp.mean(x * x, axis=-1, keepdims=True)
    return x * lax.rsqrt(ms + EPS) * g


def _sigmoid(x):
    return 1.0 / (1.0 + jnp.exp(-x))


def _swiglu(h, wg_ref, wu_ref, wd_ref):
    acc = None
    for c in range(wg_ref.shape[0]):
        g = jnp.dot(h, wg_ref[c], preferred_element_type=F32)
        u = jnp.dot(h, wu_ref[c], preferred_element_type=F32)
        a = (g * _sigmoid(g) * u).astype(BF16)
        d = jnp.dot(a, wd_ref[c], preferred_element_type=F32)
        acc = d if acc is None else acc + d
    return acc


def _ff1_kernel(x_ref, n_ref, wg_ref, wu_ref, wd_ref, o_ref):
    x = x_ref[...]
    h = _rms(x, n_ref[...]).astype(BF16)
    o_ref[...] = x + 0.5 * _swiglu(h, wg_ref, wu_ref, wd_ref)


def _ff1(x2d, norm, wg, wu, wd, tm):
    t, d = x2d.shape
    return pl.pallas_call(
        _ff1_kernel,
        out_shape=jax.ShapeDtypeStruct((t, d), F32),
        grid=(t // tm,),
        in_specs=[
            pl.BlockSpec((tm, d), lambda i: (i, 0)),
            _const_spec(norm.shape),
            _const_spec(wg.shape),
            _const_spec(wu.shape),
            _const_spec(wd.shape),
        ],
        out_specs=pl.BlockSpec((tm, d), lambda i: (i, 0)),
        compiler_params=pltpu.CompilerParams(
            dimension_semantics=("arbitrary",), vmem_limit_bytes=52 * MIB),
        name="ff1",
    )(x2d, norm, wg, wu, wd)


def _gelu(x):
    return 0.5 * x * (1.0 + lax.erf(x * (1.0 / math.sqrt(2.0))))


def _proj_kernel(x_ref, n_ref, wu_ref, wv_ref, wq_ref, wkt_ref, wvv_ref, wgt_ref, bg_ref,
                 lng_ref, lnb_ref, ws_ref, bs_ref,
                 ya_ref, q_ref, kt_ref, v_ref, g_ref):
    tm, d = x_ref.shape
    gw = d // GMLP_GROUPS
    h = _rms(x_ref[...], n_ref[...]).astype(BF16)

    u_all = _gelu(jnp.dot(h, wu_ref[...], preferred_element_type=F32))
    v_all = _gelu(jnp.dot(h, wv_ref[...], preferred_element_type=F32))
    pos_t = lax.broadcasted_iota(jnp.int32, (GMLP_BLOCK, GMLP_BLOCK), 0)
    pos_s = lax.broadcasted_iota(jnp.int32, (GMLP_BLOCK, GMLP_BLOCK), 1)
    chunk_mask = (pos_s // CHUNK) <= (pos_t // CHUNK)
    for g in range(GMLP_GROUPS):
        cols = slice(g * gw, (g + 1) * gw)
        vg = v_all[:, cols]
        mu = jnp.mean(vg, axis=-1, keepdims=True)
        dv = vg - mu
        var = jnp.mean(dv * dv, axis=-1, keepdims=True)
        vn = (dv * lax.rsqrt(var + EPS) * lng_ref[:, cols] + lnb_ref[:, cols]).astype(BF16)
        wm = jnp.where(chunk_mask, ws_ref[g], 0.0).astype(BF16)
        for r in range(tm // GMLP_BLOCK):
            rows = slice(r * GMLP_BLOCK, (r + 1) * GMLP_BLOCK)
            s_mix = jnp.dot(wm, vn[rows], preferred_element_type=F32) + bs_ref[g]
            ya_ref[rows, cols] = (u_all[rows, cols] * s_mix).astype(BF16)

    scale = LOG2E / math.sqrt(SB_HEAD_DIM)
    q_ref[...] = (jnp.dot(h, wq_ref[...], preferred_element_type=F32) * scale).astype(BF16)
    kt = lax.dot_general(wkt_ref[...], h, (((1,), (1,)), ((), ())),
                         preferred_element_type=F32).astype(BF16)
    for p in range(kt_ref.shape[1]):
        for c in range(kt_ref.shape[2]):
            kt_ref[0, p, c] = kt[p * LANES:(p + 1) * LANES, c * ATT_KC:(c + 1) * ATT_KC]
    v_ref[...] = jnp.dot(h, wvv_ref[...], preferred_element_type=F32).astype(BF16)

    g_ref[...] = _sigmoid(jnp.dot(h, wgt_ref[...], preferred_element_type=F32) + bg_ref[...])


def _proj(x1, norm, wu, wv, wq, wkt, wvv, wgt, bg, lng, lnb, ws, bs, bsz, seq, tm):
    t, d = x1.shape
    spb = seq // tm
    n_slab = d // LANES
    row = lambda i: (i, 0)
    consts = [norm, wu, wv, wq, wkt, wvv, wgt, bg, lng, lnb, ws, bs]
    return pl.pallas_call(
        _proj_kernel,
        out_shape=(
            jax.ShapeDtypeStruct((t, d), BF16),
            jax.ShapeDtypeStruct((t, d), BF16),
            jax.ShapeDtypeStruct((bsz, n_slab, seq // ATT_KC, LANES, ATT_KC), BF16),
            jax.ShapeDtypeStruct((t, d), BF16),
            jax.ShapeDtypeStruct((t, 2 * d), F32),
        ),
        grid=(t // tm,),
        in_specs=[pl.BlockSpec((tm, d), row)] + [_const_spec(a.shape) for a in consts],
        out_specs=(
            pl.BlockSpec((tm, d), row),
            pl.BlockSpec((tm, d), row),
            pl.BlockSpec((1, n_slab, tm // ATT_KC, LANES, ATT_KC),
                         lambda i: (i // spb, 0, i % spb, 0, 0)),
            pl.BlockSpec((tm, d), row),
            pl.BlockSpec((tm, 2 * d), row),
        ),
        compiler_params=pltpu.CompilerParams(
            dimension_semantics=("arbitrary",), vmem_limit_bytes=52 * MIB),
        name="proj",
    )(x1, *consts)


def _attn_kernel(q_ref, kt_ref, v_ref, o_ref, v2_ref, acc_ref, c_ref):
    seq = q_ref.shape[1]
    n_chunks = seq // ATT_KC
    n_blocks = seq // ATT_QB
    row = lax.broadcasted_iota(jnp.int32, (ATT_QB, ATT_KC), 0)
    col = lax.broadcasted_iota(jnp.int32, (ATT_QB, ATT_KC), 1)
    causal = col < row
    tri = jnp.where(row > col, 1.0, 0.0).astype(BF16)
    first_head = lax.broadcasted_iota(jnp.int32, (ATT_QB, LANES), 1) < SB_HEAD_DIM
    rows = lambda m: slice(m * ATT_QB, (m + 1) * ATT_QB)

    for j in range(n_chunks):
        vc = v_ref[0, j * ATT_KC:(j + 1) * ATT_KC, :]
        zero_v = jnp.zeros_like(vc)
        v2_ref[j, :ATT_KC, :] = jnp.where(first_head, vc, zero_v)
        v2_ref[j, ATT_KC:, :] = jnp.where(first_head, zero_v, vc)

    def sweep(units, c, acc):
        n = len(units)
        vals = [dict() for _ in range(n)]
        q2 = {}

        def scores(i):
            m, j = units[i]
            if m not in q2:
                q = q_ref[0, rows(m), :]
                zero = jnp.zeros_like(q)
                q2[m] = jnp.concatenate(
                    [jnp.where(first_head, q, zero), jnp.where(first_head, zero, q)], axis=0)
            vals[i]["z2"] = jnp.dot(q2[m], kt_ref[0, 0, j], preferred_element_type=F32)

        def softplus(i):
            m, j = units[i]
            z2 = vals[i].pop("z2")
            log_sig, sp_bf16, row_sum = [], [], []
            for h in range(HEADS_PER_SLAB):
                z = z2[h * ATT_QB:(h + 1) * ATT_QB]
                if j == m:
                    z = jnp.where(causal, z, MASKED_LOGIT)
                neg_abs = lax.bitcast_convert_type(
                    lax.bitcast_convert_type(z, jnp.uint32) | jnp.uint32(0x80000000), F32)
                sp = jnp.maximum(z, 0.0) + jnp.log(1.0 + jnp.exp2(neg_abs)) * LOG2E
                log_sig.append(z - sp)
                sp_bf16.append(sp.astype(BF16))
                row_sum.append(jnp.sum(sp, axis=1, keepdims=True))
            vals[i].update(log_sig=log_sig, sp_bf16=sp_bf16, row_sum=row_sum)

        def later_sums(i):
            vals[i]["later"] = [jnp.dot(s, tri, preferred_element_type=F32)
                                for s in vals[i].pop("sp_bf16")]

        def weights(i):
            m, j = units[i]
            v = vals[i]
            ws = []
            for h in range(HEADS_PER_SLAB):
                ws.append(jnp.exp2(v["log_sig"][h] - v["later"][h] - c[m][h]).astype(BF16))
                c[m][h] = c[m][h] + v["row_sum"][h]
            vals[i] = {"w": jnp.concatenate(ws, axis=1)}

        def weighted_values(i):
            m, j = units[i]
            d = jnp.dot(vals[i].pop("w"), v2_ref[j], preferred_element_type=F32)
            acc[m] = d if acc.get(m) is None else acc[m] + d

        for i in range(n + 2):
            if i < n:
                scores(i)
            if 0 <= i - 1 < n:
                later_sums(i - 1)
            if 0 <= i - 2 < n:
                weighted_values(i - 2)
            if i < n:
                softplus(i)
            if 0 <= i - 1 < n:
                weights(i - 1)

    blocks = range(n_blocks - 1, -1, -1)
    c = {m: [jnp.zeros((ATT_QB, 1), F32) for _ in range(HEADS_PER_SLAB)] for m in blocks}
    acc = {}
    sweep([(m, j) for m in blocks for j in (m, m - 1) if j >= 0], c, acc)
    for m in blocks:
        acc_ref[rows(m), :] = acc[m]
    unfinished = [m for m in blocks if m >= 2]
    if unfinished:
        c_min = None
        for m in unfinished:
            for h in range(HEADS_PER_SLAB):
                c_ref[h, rows(m), :] = c[m][h]
                c_min = c[m][h] if c_min is None else jnp.minimum(c_min, c[m][h])

        @pl.when(jnp.min(c_min) < ATT_SKIP_LOG2)
        def _():
            c2 = {m: [c_ref[h, rows(m), :] for h in range(HEADS_PER_SLAB)] for m in unfinished}
            acc2 = {m: acc_ref[rows(m), :] for m in unfinished}
            sweep([(m, j) for m in unfinished for j in range(m - 2, -1, -1)], c2, acc2)
            for m in unfinished:
                acc_ref[rows(m), :] = acc2[m]

    o_ref[0] = acc_ref[...].astype(BF16)


def _attn(q, kt, v):
    bsz, seq, d = q.shape
    n_slab = d // LANES
    slab = lambda b, p: (b, 0, p)
    return pl.pallas_call(
        _attn_kernel,
        out_shape=jax.ShapeDtypeStruct((bsz, seq, d), BF16),
        grid=(bsz, n_slab),
        in_specs=[
            pl.BlockSpec((1, seq, LANES), slab),
            pl.BlockSpec((1, 1, seq // ATT_KC, LANES, ATT_KC), lambda b, p: (b, p, 0, 0, 0)),
            pl.BlockSpec((1, seq, LANES), slab),
        ],
        out_specs=pl.BlockSpec((1, seq, LANES), slab),
        scratch_shapes=[pltpu.VMEM((seq // ATT_KC, HEADS_PER_SLAB * ATT_KC, LANES), BF16),
                        pltpu.VMEM((seq, LANES), F32),
                        pltpu.VMEM((HEADS_PER_SLAB, seq, 1), F32)],
        compiler_params=pltpu.CompilerParams(
            dimension_semantics=("arbitrary", "arbitrary"), vmem_limit_bytes=32 * MIB),
        name="attn",
    )(q, kt, v)


def _merge_kernel(x_ref, ya_ref, yb_ref, g_ref, wa_ref, wb_ref, wo_ref, o_ref):
    d = x_ref.shape[1]
    ma = jnp.dot(ya_ref[...], wa_ref[...], preferred_element_type=F32)
    mb = jnp.dot(yb_ref[...], wb_ref[...], preferred_element_type=F32)
    merged = (g_ref[:, :d] * ma + g_ref[:, d:] * mb).astype(BF16)
    o_ref[...] = x_ref[...] + jnp.dot(merged, wo_ref[...], preferred_element_type=F32)


def _merge(x1, ya, yb, gates, wa, wb, wo, tm):
    t, d = x1.shape
    row = lambda i: (i, 0)
    consts = [wa, wb, wo]
    return pl.pallas_call(
        _merge_kernel,
        out_shape=jax.ShapeDtypeStruct((t, d), F32),
        grid=(t // tm,),
        in_specs=[pl.BlockSpec((tm, d), row), pl.BlockSpec((tm, d), row),
                  pl.BlockSpec((tm, d), row), pl.BlockSpec((tm, 2 * d), row)]
                 + [_const_spec(a.shape) for a in consts],
        out_specs=pl.BlockSpec((tm, d), row),
        compiler_params=pltpu.CompilerParams(
            dimension_semantics=("arbitrary",), vmem_limit_bytes=52 * MIB),
        name="merge",
    )(x1, ya, yb, gates, *consts)


def _ff2_kernel(x_ref, n_ref, wg_ref, wu_ref, wd_ref, nf_ref, o_ref):
    x = x_ref[...]
    h = _rms(x, n_ref[...]).astype(BF16)
    o_ref[...] = _rms(x + 0.5 * _swiglu(h, wg_ref, wu_ref, wd_ref), nf_ref[...])


def _ff2(x2d, norm, wg, wu, wd, nf, tm):
    t, d = x2d.shape
    consts = [norm, wg, wu, wd, nf]
    return pl.pallas_call(
        _ff2_kernel,
        out_shape=jax.ShapeDtypeStruct((t, d), F32),
        grid=(t // tm,),
        in_specs=[pl.BlockSpec((tm, d), lambda i: (i, 0))] + [_const_spec(a.shape) for a in consts],
        out_specs=pl.BlockSpec((tm, d), lambda i: (i, 0)),
        compiler_params=pltpu.CompilerParams(
            dimension_semantics=("arbitrary",), vmem_limit_bytes=52 * MIB),
        name="ff2",
    )(x2d, *consts)


def _chunk_cols(w):
    d, f = w.shape
    return jnp.transpose(w.reshape(d, f // FF_CHUNK, FF_CHUNK), (1, 0, 2)).astype(BF16)


def _chunk_rows(w):
    f, d = w.shape
    return w.reshape(f // FF_CHUNK, FF_CHUNK, d).astype(BF16)


def _row_tile(seq, want):
    tm = min(want, seq)
    assert seq % tm == 0 and tm % ATT_KC == 0
    return tm


def kernel(x, ff1_norm, ff1_w_gate, ff1_w_up, ff1_w_down, mix_norm, w_in, b_gate, gmlp_ln_g, gmlp_ln_b, gmlp_w_s, gmlp_b_s, w_branch_a, w_branch_b, w_out, ff2_norm, ff2_w_gate, ff2_w_up, ff2_w_down, final_norm):
    bsz, seq, d = x.shape
    assert ff1_norm.shape[0] == 1, "the final RMSNorm is fused into the single layer's last kernel"
    assert seq % ATT_QB == 0 and d == SB_HEADS * SB_HEAD_DIM and ATT_QB == ATT_KC
    gm = d
    x2d = x.reshape(bsz * seq, d)
    x1 = _ff1(x2d, ff1_norm, _chunk_cols(ff1_w_gate[0]), _chunk_cols(ff1_w_up[0]),
              _chunk_rows(ff1_w_down[0]), _row_tile(seq, 512))

    w = w_in[0]
    o_q, o_k, o_v, o_g = 2 * gm, 2 * gm + d, 2 * gm + 2 * d, 2 * gm + 3 * d
    ya, q, kt, v, gates = _proj(
        x1, mix_norm,
        w[:, :gm].astype(BF16), w[:, gm:o_q].astype(BF16), w[:, o_q:o_k].astype(BF16),
        w[:, o_k:o_v].T.astype(BF16), w[:, o_v:o_g].astype(BF16), w[:, o_g:].astype(BF16),
        b_gate, gmlp_ln_g.reshape(1, gm), gmlp_ln_b.reshape(1, gm),
        gmlp_w_s[0], gmlp_b_s[0][:, :, None], bsz, seq, _row_tile(seq, 512))

    yb = _attn(q.reshape(bsz, seq, d), kt, v.reshape(bsz, seq, d)).reshape(bsz * seq, d)

    x2 = _merge(x1, ya, yb, gates, w_branch_a[0].astype(BF16), w_branch_b[0].astype(BF16),
                w_out[0].astype(BF16), _row_tile(seq, 512))
    y = _ff2(x2, ff2_norm, _chunk_cols(ff2_w_gate[0]), _chunk_cols(ff2_w_up[0]),
             _chunk_rows(ff2_w_down[0]), final_norm[None], _row_tile(seq, 512))
    return y.reshape(bsz, seq, d)
```

```python
import math

import jax
import jax.numpy as jnp
from jax import lax
from jax.experimental import pallas as pl
from jax.experimental.pallas import tpu as pltpu

F32 = jnp.float32
BF16 = jnp.bfloat16

EPS = 1e-6
CHUNK = 64
GMLP_BLOCK = 128
GMLP_GROUPS = 4
SB_HEADS = 16
SB_HEAD_DIM = 64
HEADS_PER_SLAB = 2
LANES = 128
FF_CHUNK = 256
ATT_QB = 256
ATT_KC = 256
MIB = 1024 * 1024
LOG2E = 1.4426950408889634
MASKED_LOGIT = -1e30
ATT_SKIP_LOG2 = 160.0
SOFTPLUS_CLAMP = 64.0


def _const_spec(shape):
    nd = len(shape)
    return pl.BlockSpec(shape, lambda *_: (0,) * nd, pipeline_mode=pl.Buffered(1))


def _rms(x, g):
    ms = jnp.mean(x * x, axis=-1, keepdims=True)
    return x * lax.rsqrt(ms + EPS) * g


def _sigmoid(x):
    return 1.0 / (1.0 + jnp.exp(-x))


def _swiglu(h, wg_ref, wu_ref, wd_ref):
    acc = None
    for c in range(wg_ref.shape[1] // FF_CHUNK):
        cols = slice(c * FF_CHUNK, (c + 1) * FF_CHUNK)
        g = jnp.dot(h, wg_ref[:, cols], preferred_element_type=F32)
        u = jnp.dot(h, wu_ref[:, cols], preferred_element_type=F32)
        a = (g * _sigmoid(g) * u).astype(BF16)
        d = jnp.dot(a, wd_ref[cols, :], preferred_element_type=F32)
        acc = d if acc is None else acc + d
    return acc


def _ff1_kernel(x_ref, n_ref, wg_ref, wu_ref, wd_ref, o_ref):
    x = x_ref[...]
    h = _rms(x, n_ref[...]).astype(BF16)
    o_ref[...] = x + 0.5 * _swiglu(h, wg_ref, wu_ref, wd_ref)


def _ff1(x2d, norm, wg, wu, wd, tm):
    t, d = x2d.shape
    return pl.pallas_call(
        _ff1_kernel,
        out_shape=jax.ShapeDtypeStruct((t, d), F32),
        grid=(t // tm,),
        in_specs=[
            pl.BlockSpec((tm, d), lambda i: (i, 0)),
            _const_spec(norm.shape),
            _const_spec(wg.shape),
            _const_spec(wu.shape),
            _const_spec(wd.shape),
        ],
        out_specs=pl.BlockSpec((tm, d), lambda i: (i, 0)),
        compiler_params=pltpu.CompilerParams(
            dimension_semantics=("arbitrary",), vmem_limit_bytes=52 * MIB),
        name="ff1",
    )(x2d, norm, wg, wu, wd)


def _gelu(x):
    return 0.5 * x * (1.0 + lax.erf(x * (1.0 / math.sqrt(2.0))))


def _proj_kernel(x_ref, n_ref, wu_ref, wv_ref, wq_ref, wkt_ref, wvv_ref, wgt_ref, bg_ref,
                 lng_ref, lnb_ref, ws_ref, bs_ref,
                 ya_ref, q_ref, kt_ref, v_ref, g_ref):
    tm, d = x_ref.shape
    gw = d // GMLP_GROUPS
    h = _rms(x_ref[...], n_ref[...]).astype(BF16)

    u_all = _gelu(jnp.dot(h, wu_ref[...], preferred_element_type=F32))
    v_all = _gelu(jnp.dot(h, wv_ref[...], preferred_element_type=F32))
    pos_t = lax.broadcasted_iota(jnp.int32, (GMLP_BLOCK, GMLP_BLOCK), 0)
    pos_s = lax.broadcasted_iota(jnp.int32, (GMLP_BLOCK, GMLP_BLOCK), 1)
    chunk_mask = (pos_s // CHUNK) <= (pos_t // CHUNK)
    for g in range(GMLP_GROUPS):
        cols = slice(g * gw, (g + 1) * gw)
        vg = v_all[:, cols]
        mu = jnp.mean(vg, axis=-1, keepdims=True)
        dv = vg - mu
        var = jnp.mean(dv * dv, axis=-1, keepdims=True)
        vn = (dv * lax.rsqrt(var + EPS) * lng_ref[:, cols] + lnb_ref[:, cols]).astype(BF16)
        wm = jnp.where(chunk_mask, ws_ref[g], 0.0).astype(BF16)
        for r in range(tm // GMLP_BLOCK):
            rows = slice(r * GMLP_BLOCK, (r + 1) * GMLP_BLOCK)
            s_mix = jnp.dot(wm, vn[rows], preferred_element_type=F32) + bs_ref[g]
            ya_ref[rows, cols] = (u_all[rows, cols] * s_mix).astype(BF16)

    scale = LOG2E / math.sqrt(SB_HEAD_DIM)
    q_ref[...] = (jnp.dot(h, wq_ref[...], preferred_element_type=F32) * scale).astype(BF16)
    kt = lax.dot_general(wkt_ref[...], h, (((1,), (1,)), ((), ())),
                         preferred_element_type=F32).astype(BF16)
    for p in range(kt_ref.shape[1]):
        for c in range(kt_ref.shape[2]):
            kt_ref[0, p, c] = kt[p * LANES:(p + 1) * LANES, c * ATT_KC:(c + 1) * ATT_KC]
    v_ref[...] = jnp.dot(h, wvv_ref[...], preferred_element_type=F32).astype(BF16)

    g_ref[...] = _sigmoid(jnp.dot(h, wgt_ref[...], preferred_element_type=F32) + bg_ref[...])


def _proj(x1, norm, wu, wv, wq, wkt, wvv, wgt, bg, lng, lnb, ws, bs, bsz, seq, tm):
    t, d = x1.shape
    spb = seq // tm
    n_slab = d // LANES
    row = lambda i: (i, 0)
    consts = [norm, wu, wv, wq, wkt, wvv, wgt, bg, lng, lnb, ws, bs]
    return pl.pallas_call(
        _proj_kernel,
        out_shape=(
            jax.ShapeDtypeStruct((t, d), BF16),
            jax.ShapeDtypeStruct((t, d), BF16),
            jax.ShapeDtypeStruct((bsz, n_slab, seq // ATT_KC, LANES, ATT_KC), BF16),
            jax.ShapeDtypeStruct((t, d), BF16),
            jax.ShapeDtypeStruct((t, 2 * d), F32),
        ),
        grid=(t // tm,),
        in_specs=[pl.BlockSpec((tm, d), row)] + [_const_spec(a.shape) for a in consts],
        out_specs=(
            pl.BlockSpec((tm, d), row),
            pl.BlockSpec((tm, d), row),
            pl.BlockSpec((1, n_slab, tm // ATT_KC, LANES, ATT_KC),
                         lambda i: (i // spb, 0, i % spb, 0, 0)),
            pl.BlockSpec((tm, d), row),
            pl.BlockSpec((tm, 2 * d), row),
        ),
        compiler_params=pltpu.CompilerParams(
            dimension_semantics=("arbitrary",), vmem_limit_bytes=52 * MIB),
        name="proj",
    )(x1, *consts)


def _attn_kernel(q_ref, kt_ref, v_ref, o_ref, v2_ref, acc_ref, c_ref):
    seq = q_ref.shape[1]
    n_chunks = seq // ATT_KC
    n_blocks = seq // ATT_QB
    row = lax.broadcasted_iota(jnp.int32, (ATT_QB, ATT_KC), 0)
    col = lax.broadcasted_iota(jnp.int32, (ATT_QB, ATT_KC), 1)
    causal = col < row
    tri = jnp.where(row > col, 1.0, 0.0).astype(BF16)
    first_head = lax.broadcasted_iota(jnp.int32, (ATT_QB, LANES), 1) < SB_HEAD_DIM
    rows = lambda m: slice(m * ATT_QB, (m + 1) * ATT_QB)

    for j in range(n_chunks):
        vc = v_ref[0, j * ATT_KC:(j + 1) * ATT_KC, :]
        zero_v = jnp.zeros_like(vc)
        v2_ref[j, :ATT_KC, :] = jnp.where(first_head, vc, zero_v)
        v2_ref[j, ATT_KC:, :] = jnp.where(first_head, zero_v, vc)

    def sweep(units, c, acc):
        n = len(units)
        vals = [dict() for _ in range(n)]
        q2 = {}

        def scores(i):
            m, j = units[i]
            if m not in q2:
                q = q_ref[0, rows(m), :]
                zero = jnp.zeros_like(q)
                q2[m] = jnp.concatenate(
                    [jnp.where(first_head, q, zero), jnp.where(first_head, zero, q)], axis=0)
            vals[i]["z2"] = jnp.dot(q2[m], kt_ref[0, 0, j], preferred_element_type=F32)

        def softplus(i):
            m, j = units[i]
            z2 = vals[i].pop("z2")
            log_sig, sp_bf16, row_sum = [], [], []
            for h in range(HEADS_PER_SLAB):
                z = z2[h * ATT_QB:(h + 1) * ATT_QB]
                if j == m:
                    z = jnp.where(causal, z, MASKED_LOGIT)
                sp = jnp.maximum(
                    jnp.log(1.0 + jnp.exp2(jnp.minimum(z, SOFTPLUS_CLAMP))) * LOG2E, z)
                log_sig.append(z - sp)
                sp_bf16.append(sp.astype(BF16))
                row_sum.append(jnp.sum(sp, axis=1, keepdims=True))
            vals[i].update(log_sig=log_sig, sp_bf16=sp_bf16, row_sum=row_sum)

        def later_sums(i):
            vals[i]["later"] = [jnp.dot(s, tri, preferred_element_type=F32)
                                for s in vals[i].pop("sp_bf16")]

        def weights(i):
            m, j = units[i]
            v = vals[i]
            ws = []
            for h in range(HEADS_PER_SLAB):
                ws.append(jnp.exp2(v["log_sig"][h] - v["later"][h] - c[m][h]).astype(BF16))
                c[m][h] = c[m][h] + v["row_sum"][h]
            vals[i] = {"w": jnp.concatenate(ws, axis=1)}

        def weighted_values(i):
            m, j = units[i]
            d = jnp.dot(vals[i].pop("w"), v2_ref[j], preferred_element_type=F32)
            acc[m] = d if acc.get(m) is None else acc[m] + d

        for i in range(n + 2):
            if i < n:
                scores(i)
            if 0 <= i - 1 < n:
                later_sums(i - 1)
            if 0 <= i - 2 < n:
                weighted_values(i - 2)
            if i < n:
                softplus(i)
            if 0 <= i - 1 < n:
                weights(i - 1)

    blocks = range(n_blocks - 1, -1, -1)
    c = {m: [jnp.zeros((ATT_QB, 1), F32) for _ in range(HEADS_PER_SLAB)] for m in blocks}
    acc = {}
    sweep([(m, j) for m in blocks for j in (m, m - 1) if j >= 0], c, acc)
    for m in blocks:
        acc_ref[rows(m), :] = acc[m]
    unfinished = [m for m in blocks if m >= 2]
    if unfinished:
        c_min = None
        for m in unfinished:
            for h in range(HEADS_PER_SLAB):
                c_ref[h, rows(m), :] = c[m][h]
                c_min = c[m][h] if c_min is None else jnp.minimum(c_min, c[m][h])

        @pl.when(jnp.min(c_min) < ATT_SKIP_LOG2)
        def _():
            c2 = {m: [c_ref[h, rows(m), :] for h in range(HEADS_PER_SLAB)] for m in unfinished}
            acc2 = {m: acc_ref[rows(m), :] for m in unfinished}
            sweep([(m, j) for m in unfinished for j in range(m - 2, -1, -1)], c2, acc2)
            for m in unfinished:
                acc_ref[rows(m), :] = acc2[m]

    o_ref[0] = acc_ref[...].astype(BF16)


def _attn(q, kt, v):
    bsz, seq, d = q.shape
    n_slab = d // LANES
    slab = lambda b, p: (b, 0, p)
    return pl.pallas_call(
        _attn_kernel,
        out_shape=jax.ShapeDtypeStruct((bsz, seq, d), BF16),
        grid=(bsz, n_slab),
        in_specs=[
            pl.BlockSpec((1, seq, LANES), slab),
            pl.BlockSpec((1, 1, seq // ATT_KC, LANES, ATT_KC), lambda b, p: (b, p, 0, 0, 0)),
            pl.BlockSpec((1, seq, LANES), slab),
        ],
        out_specs=pl.BlockSpec((1, seq, LANES), slab),
        scratch_shapes=[pltpu.VMEM((seq // ATT_KC, HEADS_PER_SLAB * ATT_KC, LANES), BF16),
                        pltpu.VMEM((seq, LANES), F32),
                        pltpu.VMEM((HEADS_PER_SLAB, seq, 1), F32)],
        compiler_params=pltpu.CompilerParams(
            dimension_semantics=("arbitrary", "arbitrary"), vmem_limit_bytes=32 * MIB),
        name="attn",
    )(q, kt, v)


def _merge_kernel(x_ref, ya_ref, yb_ref, g_ref, wa_ref, wb_ref, wo_ref, o_ref):
    d = x_ref.shape[1]
    ma = jnp.dot(ya_ref[...], wa_ref[...], preferred_element_type=F32)
    mb = jnp.dot(yb_ref[...], wb_ref[...], preferred_element_type=F32)
    merged = (g_ref[:, :d] * ma + g_ref[:, d:] * mb).astype(BF16)
    o_ref[...] = x_ref[...] + jnp.dot(merged, wo_ref[...], preferred_element_type=F32)


def _merge(x1, ya, yb, gates, wa, wb, wo, tm):
    t, d = x1.shape
    row = lambda i: (i, 0)
    consts = [wa, wb, wo]
    return pl.pallas_call(
        _merge_kernel,
        out_shape=jax.ShapeDtypeStruct((t, d), F32),
        grid=(t // tm,),
        in_specs=[pl.BlockSpec((tm, d), row), pl.BlockSpec((tm, d), row),
                  pl.BlockSpec((tm, d), row), pl.BlockSpec((tm, 2 * d), row)]
                 + [_const_spec(a.shape) for a in consts],
        out_specs=pl.BlockSpec((tm, d), row),
        compiler_params=pltpu.CompilerParams(
            dimension_semantics=("arbitrary",), vmem_limit_bytes=52 * MIB),
        name="merge",
    )(x1, ya, yb, gates, *consts)


def _ff2_kernel(x_ref, n_ref, wg_ref, wu_ref, wd_ref, nf_ref, o_ref):
    x = x_ref[...]
    h = _rms(x, n_ref[...]).astype(BF16)
    o_ref[...] = _rms(x + 0.5 * _swiglu(h, wg_ref, wu_ref, wd_ref), nf_ref[...])


def _ff2(x2d, norm, wg, wu, wd, nf, tm):
    t, d = x2d.shape
    consts = [norm, wg, wu, wd, nf]
    return pl.pallas_call(
        _ff2_kernel,
        out_shape=jax.ShapeDtypeStruct((t, d), F32),
        grid=(t // tm,),
        in_specs=[pl.BlockSpec((tm, d), lambda i: (i, 0))] + [_const_spec(a.shape) for a in consts],
        out_specs=pl.BlockSpec((tm, d), lambda i: (i, 0)),
        compiler_params=pltpu.CompilerParams(
            dimension_semantics=("arbitrary",), vmem_limit_bytes=52 * MIB),
        name="ff2",
    )(x2d, *consts)


def _row_tile(seq, want):
    tm = min(want, seq)
    assert seq % tm == 0 and tm % ATT_KC == 0
    return tm


def kernel(x, ff1_norm, ff1_w_gate, ff1_w_up, ff1_w_down, mix_norm, w_in, b_gate, gmlp_ln_g, gmlp_ln_b, gmlp_w_s, gmlp_b_s, w_branch_a, w_branch_b, w_out, ff2_norm, ff2_w_gate, ff2_w_up, ff2_w_down, final_norm):
    bsz, seq, d = x.shape
    assert ff1_norm.shape[0] == 1, "the final RMSNorm is fused into the single layer's last kernel"
    assert seq % ATT_QB == 0 and d == SB_HEADS * SB_HEAD_DIM and ATT_QB == ATT_KC
    gm = d
    x2d = x.reshape(bsz * seq, d)
    assert ff1_w_gate.shape[2] % FF_CHUNK == 0
    x1 = _ff1(x2d, ff1_norm, ff1_w_gate[0].astype(BF16), ff1_w_up[0].astype(BF16),
              ff1_w_down[0].astype(BF16), _row_tile(seq, 512))

    w = w_in[0]
    o_q, o_k, o_v, o_g = 2 * gm, 2 * gm + d, 2 * gm + 2 * d, 2 * gm + 3 * d
    ya, q, kt, v, gates = _proj(
        x1, mix_norm,
        w[:, :gm].astype(BF16), w[:, gm:o_q].astype(BF16), w[:, o_q:o_k].astype(BF16),
        w[:, o_k:o_v].T.astype(BF16), w[:, o_v:o_g].astype(BF16), w[:, o_g:].astype(BF16),
        b_gate, gmlp_ln_g.reshape(1, gm), gmlp_ln_b.reshape(1, gm),
        gmlp_w_s[0], gmlp_b_s[0][:, :, None], bsz, seq, _row_tile(seq, 512))

    yb = _attn(q.reshape(bsz, seq, d), kt, v.reshape(bsz, seq, d)).reshape(bsz * seq, d)

    x2 = _merge(x1, ya, yb, gates, w_branch_a[0].astype(BF16), w_branch_b[0].astype(BF16),
                w_out[0].astype(BF16), _row_tile(seq, 512))
    y = _ff2(x2, ff2_norm, ff2_w_gate[0].astype(BF16), ff2_w_up[0].astype(BF16),
             ff2_w_down[0].astype(BF16), final_norm[None], _row_tile(seq, 512))
    return y.reshape(bsz, seq, d)
```

```python
import math

import jax
import jax.numpy as jnp
from jax import lax
from jax.experimental import pallas as pl
from jax.experimental.pallas import tpu as pltpu

F32 = jnp.float32
BF16 = jnp.bfloat16

EPS = 1e-6
CHUNK = 64
GMLP_BLOCK = 128
GMLP_GROUPS = 4
SB_HEADS = 16
SB_HEAD_DIM = 64
HEADS_PER_SLAB = 2
LANES = 128
FF_CHUNK = 256
ATT_QB = 256
ATT_KC = 256
MIB = 1024 * 1024
LOG2E = 1.4426950408889634
MASKED_LOGIT = -1e30
ATT_SKIP_LOG2 = 160.0
SOFTPLUS_CLAMP = 64.0


def _const_spec(shape):
    nd = len(shape)
    return pl.BlockSpec(shape, lambda *_: (0,) * nd, pipeline_mode=pl.Buffered(1))


def _rms(x, g):
    ms = jnp.mean(x * x, axis=-1, keepdims=True)
    return x * lax.rsqrt(ms + EPS) * g


def _sigmoid(x):
    return 1.0 / (1.0 + jnp.exp(-x))


def _swiglu(h, wg_ref, wu_ref, wd_ref):
    acc = None
    for c in range(wg_ref.shape[1] // FF_CHUNK):
        cols = slice(c * FF_CHUNK, (c + 1) * FF_CHUNK)
        g = jnp.dot(h, wg_ref[:, cols], preferred_element_type=F32)
        u = jnp.dot(h, wu_ref[:, cols], preferred_element_type=F32)
        a = (g * _sigmoid(g) * u).astype(BF16)
        d = jnp.dot(a, wd_ref[cols, :], preferred_element_type=F32)
        acc = d if acc is None else acc + d
    return acc


def _ff1_kernel(x_ref, n_ref, wg_ref, wu_ref, wd_ref, o_ref):
    x = x_ref[...]
    h = _rms(x, n_ref[...]).astype(BF16)
    o_ref[...] = x + 0.5 * _swiglu(h, wg_ref, wu_ref, wd_ref)


def _ff1(x2d, norm, wg, wu, wd, tm):
    t, d = x2d.shape
    return pl.pallas_call(
        _ff1_kernel,
        out_shape=jax.ShapeDtypeStruct((t, d), F32),
        grid=(t // tm,),
        in_specs=[
            pl.BlockSpec((tm, d), lambda i: (i, 0)),
            _const_spec(norm.shape),
            _const_spec(wg.shape),
            _const_spec(wu.shape),
            _const_spec(wd.shape),
        ],
        out_specs=pl.BlockSpec((tm, d), lambda i: (i, 0)),
        compiler_params=pltpu.CompilerParams(
            dimension_semantics=("arbitrary",), vmem_limit_bytes=52 * MIB),
        name="ff1",
    )(x2d, norm, wg, wu, wd)


def _gelu(x):
    return 0.5 * x * (1.0 + lax.erf(x * (1.0 / math.sqrt(2.0))))


def _proj_kernel(x_ref, n_ref, wu_ref, wv_ref, wq_ref, wkt_ref, wvt_ref, wgt_ref, bg_ref,
                 lng_ref, lnb_ref, ws_ref, bs_ref,
                 ya_ref, q_ref, kt_ref, vt_ref, g_ref):
    tm, d = x_ref.shape
    gw = d // GMLP_GROUPS
    h = _rms(x_ref[...], n_ref[...]).astype(BF16)

    u_all = _gelu(jnp.dot(h, wu_ref[...], preferred_element_type=F32))
    v_all = _gelu(jnp.dot(h, wv_ref[...], preferred_element_type=F32))
    pos_t = lax.broadcasted_iota(jnp.int32, (GMLP_BLOCK, GMLP_BLOCK), 0)
    pos_s = lax.broadcasted_iota(jnp.int32, (GMLP_BLOCK, GMLP_BLOCK), 1)
    chunk_mask = (pos_s // CHUNK) <= (pos_t // CHUNK)
    for g in range(GMLP_GROUPS):
        cols = slice(g * gw, (g + 1) * gw)
        vg = v_all[:, cols]
        mu = jnp.mean(vg, axis=-1, keepdims=True)
        dv = vg - mu
        var = jnp.mean(dv * dv, axis=-1, keepdims=True)
        vn = (dv * lax.rsqrt(var + EPS) * lng_ref[:, cols] + lnb_ref[:, cols]).astype(BF16)
        wm = jnp.where(chunk_mask, ws_ref[g], 0.0).astype(BF16)
        for r in range(tm // GMLP_BLOCK):
            rows = slice(r * GMLP_BLOCK, (r + 1) * GMLP_BLOCK)
            s_mix = jnp.dot(wm, vn[rows], preferred_element_type=F32) + bs_ref[g]
            ya_ref[rows, cols] = (u_all[rows, cols] * s_mix).astype(BF16)

    scale = LOG2E / math.sqrt(SB_HEAD_DIM)
    q_ref[...] = (jnp.dot(h, wq_ref[...], preferred_element_type=F32) * scale).astype(BF16)
    for wt_ref, out_ref in ((wkt_ref, kt_ref), (wvt_ref, vt_ref)):
        tr = lax.dot_general(wt_ref[...], h, (((1,), (1,)), ((), ())),
                             preferred_element_type=F32).astype(BF16)
        for p in range(out_ref.shape[1]):
            for c in range(out_ref.shape[2]):
                out_ref[0, p, c] = tr[p * LANES:(p + 1) * LANES, c * ATT_KC:(c + 1) * ATT_KC]

    g_ref[...] = _sigmoid(
        jnp.dot(h, wgt_ref[...], preferred_element_type=F32) + bg_ref[...]).astype(BF16)


def _proj(x1, norm, wu, wv, wq, wkt, wvt, wgt, bg, lng, lnb, ws, bs, bsz, seq, tm):
    t, d = x1.shape
    spb = seq // tm
    n_slab = d // LANES
    row = lambda i: (i, 0)
    consts = [norm, wu, wv, wq, wkt, wvt, wgt, bg, lng, lnb, ws, bs]
    tiles = jax.ShapeDtypeStruct((bsz, n_slab, seq // ATT_KC, LANES, ATT_KC), BF16)
    tile_spec = pl.BlockSpec((1, n_slab, tm // ATT_KC, LANES, ATT_KC),
                             lambda i: (i // spb, 0, i % spb, 0, 0))
    return pl.pallas_call(
        _proj_kernel,
        out_shape=(
            jax.ShapeDtypeStruct((t, d), BF16),
            jax.ShapeDtypeStruct((t, d), BF16),
            tiles,
            tiles,
            jax.ShapeDtypeStruct((t, 2 * d), BF16),
        ),
        grid=(t // tm,),
        in_specs=[pl.BlockSpec((tm, d), row)] + [_const_spec(a.shape) for a in consts],
        out_specs=(
            pl.BlockSpec((tm, d), row),
            pl.BlockSpec((tm, d), row),
            tile_spec,
            tile_spec,
            pl.BlockSpec((tm, 2 * d), row),
        ),
        compiler_params=pltpu.CompilerParams(
            dimension_semantics=("arbitrary",), vmem_limit_bytes=52 * MIB),
        name="proj",
    )(x1, *consts)


def _attn_kernel(q_ref, kt_ref, vt_ref, o_ref, v2_ref, acc_ref, c_ref):
    seq = q_ref.shape[1]
    n_chunks = seq // ATT_KC
    n_blocks = seq // ATT_QB
    row = lax.broadcasted_iota(jnp.int32, (ATT_QB, ATT_KC), 0)
    col = lax.broadcasted_iota(jnp.int32, (ATT_QB, ATT_KC), 1)
    causal = col < row
    tri = jnp.where(row > col, 1.0, 0.0).astype(BF16)
    first_head = lax.broadcasted_iota(jnp.int32, (ATT_QB, LANES), 1) < SB_HEAD_DIM
    rows = lambda m: slice(m * ATT_QB, (m + 1) * ATT_QB)

    first_head_rows = lax.broadcasted_iota(jnp.int32, (LANES, ATT_KC), 0) < SB_HEAD_DIM
    for j in range(n_chunks):
        vc = vt_ref[0, 0, j]
        zero_v = jnp.zeros_like(vc)
        v2_ref[j, :, :ATT_KC] = jnp.where(first_head_rows, vc, zero_v)
        v2_ref[j, :, ATT_KC:] = jnp.where(first_head_rows, zero_v, vc)

    def sweep(units, c, acc):
        n = len(units)
        vals = [dict() for _ in range(n)]
        q2 = {}

        def scores(i):
            m, j = units[i]
            if m not in q2:
                q = q_ref[0, rows(m), :]
                zero = jnp.zeros_like(q)
                q2[m] = jnp.concatenate(
                    [jnp.where(first_head, q, zero), jnp.where(first_head, zero, q)], axis=0)
            vals[i]["z2"] = jnp.dot(q2[m], kt_ref[0, 0, j], preferred_element_type=F32)

        def softplus(i):
            m, j = units[i]
            z2 = vals[i].pop("z2")
            log_sig, sp_bf16, row_sum = [], [], []
            for h in range(HEADS_PER_SLAB):
                z = z2[h * ATT_QB:(h + 1) * ATT_QB]
                if j == m:
                    z = jnp.where(causal, z, MASKED_LOGIT)
                sp = jnp.maximum(
                    jnp.log(1.0 + jnp.exp2(jnp.minimum(z, SOFTPLUS_CLAMP))) * LOG2E, z)
                log_sig.append(z - sp)
                sp_bf16.append(sp.astype(BF16))
                row_sum.append(jnp.sum(sp, axis=1, keepdims=True))
            vals[i].update(log_sig=log_sig, sp_bf16=sp_bf16, row_sum=row_sum)

        def later_sums(i):
            vals[i]["later"] = [jnp.dot(s, tri, preferred_element_type=F32)
                                for s in vals[i].pop("sp_bf16")]

        def weights(i):
            m, j = units[i]
            v = vals[i]
            ws = []
            for h in range(HEADS_PER_SLAB):
                ws.append(jnp.exp2(v["log_sig"][h] - v["later"][h] - c[m][h]).astype(BF16))
                c[m][h] = c[m][h] + v["row_sum"][h]
            vals[i] = {"w": jnp.concatenate(ws, axis=1)}

        def weighted_values(i):
            m, j = units[i]
            d = lax.dot_general(v2_ref[j], vals[i].pop("w"), (((1,), (1,)), ((), ())),
                                preferred_element_type=F32)
            acc[m] = d if acc.get(m) is None else acc[m] + d

        for i in range(n + 2):
            if i < n:
                scores(i)
            if 0 <= i - 1 < n:
                later_sums(i - 1)
            if 0 <= i - 2 < n:
                weighted_values(i - 2)
            if i < n:
                softplus(i)
            if 0 <= i - 1 < n:
                weights(i - 1)

    blocks = range(n_blocks - 1, -1, -1)
    c = {m: [jnp.zeros((ATT_QB, 1), F32) for _ in range(HEADS_PER_SLAB)] for m in blocks}
    acc = {}
    sweep([(m, j) for m in blocks for j in (m, m - 1) if j >= 0], c, acc)
    for m in blocks:
        acc_ref[:, rows(m)] = acc[m]
    unfinished = [m for m in blocks if m >= 2]
    if unfinished:
        c_min = None
        for m in unfinished:
            for h in range(HEADS_PER_SLAB):
                c_ref[h, rows(m), :] = c[m][h]
                c_min = c[m][h] if c_min is None else jnp.minimum(c_min, c[m][h])

        @pl.when(jnp.min(c_min) < ATT_SKIP_LOG2)
        def _():
            c2 = {m: [c_ref[h, rows(m), :] for h in range(HEADS_PER_SLAB)] for m in unfinished}
            acc2 = {m: acc_ref[:, rows(m)] for m in unfinished}
            sweep([(m, j) for m in unfinished for j in range(m - 2, -1, -1)], c2, acc2)
            for m in unfinished:
                acc_ref[:, rows(m)] = acc2[m]

    o_ref[0] = acc_ref[...].T.astype(BF16)


def _attn(q, kt, vt):
    bsz, seq, d = q.shape
    n_slab = d // LANES
    slab = lambda b, p: (b, 0, p)
    tiles = pl.BlockSpec((1, 1, seq // ATT_KC, LANES, ATT_KC), lambda b, p: (b, p, 0, 0, 0))
    return pl.pallas_call(
        _attn_kernel,
        out_shape=jax.ShapeDtypeStruct((bsz, seq, d), BF16),
        grid=(bsz, n_slab),
        in_specs=[pl.BlockSpec((1, seq, LANES), slab), tiles, tiles],
        out_specs=pl.BlockSpec((1, seq, LANES), slab),
        scratch_shapes=[pltpu.VMEM((seq // ATT_KC, LANES, HEADS_PER_SLAB * ATT_KC), BF16),
                        pltpu.VMEM((LANES, seq), F32),
                        pltpu.VMEM((HEADS_PER_SLAB, seq, 1), F32)],
        compiler_params=pltpu.CompilerParams(
            dimension_semantics=("arbitrary", "arbitrary"), vmem_limit_bytes=32 * MIB),
        name="attn",
    )(q, kt, vt)


def _merge_kernel(x_ref, ya_ref, yb_ref, g_ref, wa_ref, wb_ref, wo_ref, o_ref):
    d = x_ref.shape[1]
    ma = jnp.dot(ya_ref[...], wa_ref[...], preferred_element_type=F32)
    mb = jnp.dot(yb_ref[...], wb_ref[...], preferred_element_type=F32)
    merged = (g_ref[:, :d] * ma + g_ref[:, d:] * mb).astype(BF16)
    o_ref[...] = x_ref[...] + jnp.dot(merged, wo_ref[...], preferred_element_type=F32)


def _merge(x1, ya, yb, gates, wa, wb, wo, tm):
    t, d = x1.shape
    row = lambda i: (i, 0)
    consts = [wa, wb, wo]
    return pl.pallas_call(
        _merge_kernel,
        out_shape=jax.ShapeDtypeStruct((t, d), F32),
        grid=(t // tm,),
        in_specs=[pl.BlockSpec((tm, d), row), pl.BlockSpec((tm, d), row),
                  pl.BlockSpec((tm, d), row), pl.BlockSpec((tm, 2 * d), row)]
                 + [_const_spec(a.shape) for a in consts],
        out_specs=pl.BlockSpec((tm, d), row),
        compiler_params=pltpu.CompilerParams(
            dimension_semantics=("arbitrary",), vmem_limit_bytes=52 * MIB),
        name="merge",
    )(x1, ya, yb, gates, *consts)


def _ff2_kernel(x_ref, n_ref, wg_ref, wu_ref, wd_ref, nf_ref, o_ref):
    x = x_ref[...]
    h = _rms(x, n_ref[...]).astype(BF16)
    o_ref[...] = _rms(x + 0.5 * _swiglu(h, wg_ref, wu_ref, wd_ref), nf_ref[...])


def _ff2(x2d, norm, wg, wu, wd, nf, tm):
    t, d = x2d.shape
    consts = [norm, wg, wu, wd, nf]
    return pl.pallas_call(
        _ff2_kernel,
        out_shape=jax.ShapeDtypeStruct((t, d), F32),
        grid=(t // tm,),
        in_specs=[pl.BlockSpec((tm, d), lambda i: (i, 0))] + [_const_spec(a.shape) for a in consts],
        out_specs=pl.BlockSpec((tm, d), lambda i: (i, 0)),
        compiler_params=pltpu.CompilerParams(
            dimension_semantics=("arbitrary",), vmem_limit_bytes=52 * MIB),
        name="ff2",
    )(x2d, *consts)


def _row_tile(seq, want):
    tm = min(want, seq)
    assert seq % tm == 0 and tm % ATT_KC == 0
    return tm


def kernel(x, ff1_norm, ff1_w_gate, ff1_w_up, ff1_w_down, mix_norm, w_in, b_gate, gmlp_ln_g, gmlp_ln_b, gmlp_w_s, gmlp_b_s, w_branch_a, w_branch_b, w_out, ff2_norm, ff2_w_gate, ff2_w_up, ff2_w_down, final_norm):
    bsz, seq, d = x.shape
    assert ff1_norm.shape[0] == 1, "the final RMSNorm is fused into the single layer's last kernel"
    assert seq % ATT_QB == 0 and d == SB_HEADS * SB_HEAD_DIM and ATT_QB == ATT_KC
    gm = d
    x2d = x.reshape(bsz * seq, d)
    assert ff1_w_gate.shape[2] % FF_CHUNK == 0
    x1 = _ff1(x2d, ff1_norm, ff1_w_gate[0].astype(BF16), ff1_w_up[0].astype(BF16),
              ff1_w_down[0].astype(BF16), _row_tile(seq, 512))

    w = w_in[0]
    o_q, o_k, o_v, o_g = 2 * gm, 2 * gm + d, 2 * gm + 2 * d, 2 * gm + 3 * d
    ya, q, kt, vt, gates = _proj(
        x1, mix_norm,
        w[:, :gm].astype(BF16), w[:, gm:o_q].astype(BF16), w[:, o_q:o_k].astype(BF16),
        w[:, o_k:o_v].T.astype(BF16), w[:, o_v:o_g].T.astype(BF16), w[:, o_g:].astype(BF16),
        b_gate, gmlp_ln_g.reshape(1, gm), gmlp_ln_b.reshape(1, gm),
        gmlp_w_s[0], gmlp_b_s[0][:, :, None], bsz, seq, _row_tile(seq, 512))

    yb = _attn(q.reshape(bsz, seq, d), kt, vt).reshape(bsz * seq, d)

    x2 = _merge(x1, ya, yb, gates, w_branch_a[0].astype(BF16), w_branch_b[0].astype(BF16),
                w_out[0].astype(BF16), _row_tile(seq, 512))
    y = _ff2(x2, ff2_norm, ff2_w_gate[0].astype(BF16), ff2_w_up[0].astype(BF16),
             ff2_w_down[0].astype(BF16), final_norm[None], _row_tile(seq, 512))
    return y.reshape(bsz, seq, d)
```

```python
import math

import jax
import jax.numpy as jnp
from jax import lax
from jax.experimental import pallas as pl
from jax.experimental.pallas import tpu as pltpu

F32 = jnp.float32
BF16 = jnp.bfloat16

EPS = 1e-6
CHUNK = 64
GMLP_BLOCK = 128
GMLP_GROUPS = 4
SB_HEADS = 16
SB_HEAD_DIM = 64
HEADS_PER_SLAB = 2
LANES = 128
FF_CHUNK = 256
ATT_QB = 256
ATT_KC = 256
MIB = 1024 * 1024
LOG2E = 1.4426950408889634
MASKED_LOGIT = -1e30
ATT_SKIP_LOG2 = 160.0
SOFTPLUS_CLAMP = 64.0


def _const_spec(shape):
    nd = len(shape)
    return pl.BlockSpec(shape, lambda *_: (0,) * nd, pipeline_mode=pl.Buffered(1))


def _rms(x, g):
    ms = jnp.mean(x * x, axis=-1, keepdims=True)
    return x * lax.rsqrt(ms + EPS) * g


def _sigmoid(x):
    return 1.0 / (1.0 + jnp.exp(-x))


def _swiglu(h, wg_ref, wu_ref, wd_ref):
    acc = None
    for c in range(wg_ref.shape[1] // FF_CHUNK):
        cols = slice(c * FF_CHUNK, (c + 1) * FF_CHUNK)
        g = jnp.dot(h, wg_ref[:, cols], preferred_element_type=F32)
        u = jnp.dot(h, wu_ref[:, cols], preferred_element_type=F32)
        a = (g * _sigmoid(g) * u).astype(BF16)
        d = jnp.dot(a, wd_ref[cols, :], preferred_element_type=F32)
        acc = d if acc is None else acc + d
    return acc


def _ff1_kernel(x_ref, n_ref, wg_ref, wu_ref, wd_ref, o_ref):
    x = x_ref[...]
    h = _rms(x, n_ref[...]).astype(BF16)
    o_ref[...] = x + 0.5 * _swiglu(h, wg_ref, wu_ref, wd_ref)


def _ff1(x2d, norm, wg, wu, wd, tm):
    t, d = x2d.shape
    return pl.pallas_call(
        _ff1_kernel,
        out_shape=jax.ShapeDtypeStruct((t, d), F32),
        grid=(t // tm,),
        in_specs=[
            pl.BlockSpec((tm, d), lambda i: (i, 0)),
            _const_spec(norm.shape),
            _const_spec(wg.shape),
            _const_spec(wu.shape),
            _const_spec(wd.shape),
        ],
        out_specs=pl.BlockSpec((tm, d), lambda i: (i, 0)),
        compiler_params=pltpu.CompilerParams(
            dimension_semantics=("arbitrary",), vmem_limit_bytes=52 * MIB),
        name="ff1",
    )(x2d, norm, wg, wu, wd)


def _gelu(x):
    return 0.5 * x * (1.0 + lax.erf(x * (1.0 / math.sqrt(2.0))))


def _proj_kernel(x_ref, n_ref, wu_ref, wv_ref, wq_ref, wkt_ref, wvt_ref, wgt_ref, bg_ref,
                 lng_ref, lnb_ref, ws_ref, bs_ref,
                 ya_ref, q_ref, kt_ref, vt_ref, g_ref):
    tm, d = x_ref.shape
    gw = d // GMLP_GROUPS
    h = _rms(x_ref[...], n_ref[...]).astype(BF16)

    g_ref[...] = _sigmoid(
        jnp.dot(h, wgt_ref[...], preferred_element_type=F32) + bg_ref[...]).astype(BF16)
    u_all = _gelu(jnp.dot(h, wu_ref[...], preferred_element_type=F32))
    v_all = _gelu(jnp.dot(h, wv_ref[...], preferred_element_type=F32))

    scale = LOG2E / math.sqrt(SB_HEAD_DIM)
    q_ref[...] = (jnp.dot(h, wq_ref[...], preferred_element_type=F32) * scale).astype(BF16)
    for wt_ref, out_ref in ((wkt_ref, kt_ref), (wvt_ref, vt_ref)):
        tr = lax.dot_general(wt_ref[...], h, (((1,), (1,)), ((), ())),
                             preferred_element_type=F32).astype(BF16)
        for p in range(out_ref.shape[1]):
            for c in range(out_ref.shape[2]):
                out_ref[0, p, c] = tr[p * LANES:(p + 1) * LANES, c * ATT_KC:(c + 1) * ATT_KC]

    pos_t = lax.broadcasted_iota(jnp.int32, (GMLP_BLOCK, GMLP_BLOCK), 0)
    pos_s = lax.broadcasted_iota(jnp.int32, (GMLP_BLOCK, GMLP_BLOCK), 1)
    chunk_mask = (pos_s // CHUNK) <= (pos_t // CHUNK)
    for g in range(GMLP_GROUPS):
        cols = slice(g * gw, (g + 1) * gw)
        vg = v_all[:, cols]
        mu = jnp.mean(vg, axis=-1, keepdims=True)
        dv = vg - mu
        var = jnp.mean(dv * dv, axis=-1, keepdims=True)
        vn = (dv * lax.rsqrt(var + EPS) * lng_ref[:, cols] + lnb_ref[:, cols]).astype(BF16)
        wm = jnp.where(chunk_mask, ws_ref[g], 0.0).astype(BF16)
        for r in range(tm // GMLP_BLOCK):
            rows = slice(r * GMLP_BLOCK, (r + 1) * GMLP_BLOCK)
            s_mix = jnp.dot(wm, vn[rows], preferred_element_type=F32) + bs_ref[g]
            ya_ref[rows, cols] = (u_all[rows, cols] * s_mix).astype(BF16)


def _proj(x1, norm, wu, wv, wq, wkt, wvt, wgt, bg, lng, lnb, ws, bs, bsz, seq, tm):
    t, d = x1.shape
    spb = seq // tm
    n_slab = d // LANES
    row = lambda i: (i, 0)
    consts = [norm, wu, wv, wq, wkt, wvt, wgt, bg, lng, lnb, ws, bs]
    tiles = jax.ShapeDtypeStruct((bsz, n_slab, seq // ATT_KC, LANES, ATT_KC), BF16)
    tile_spec = pl.BlockSpec((1, n_slab, tm // ATT_KC, LANES, ATT_KC),
                             lambda i: (i // spb, 0, i % spb, 0, 0))
    return pl.pallas_call(
        _proj_kernel,
        out_shape=(
            jax.ShapeDtypeStruct((t, d), BF16),
            jax.ShapeDtypeStruct((t, d), BF16),
            tiles,
            tiles,
            jax.ShapeDtypeStruct((t, 2 * d), BF16),
        ),
        grid=(t // tm,),
        in_specs=[pl.BlockSpec((tm, d), row)] + [_const_spec(a.shape) for a in consts],
        out_specs=(
            pl.BlockSpec((tm, d), row),
            pl.BlockSpec((tm, d), row),
            tile_spec,
            tile_spec,
            pl.BlockSpec((tm, 2 * d), row),
        ),
        compiler_params=pltpu.CompilerParams(
            dimension_semantics=("arbitrary",), vmem_limit_bytes=52 * MIB),
        name="proj",
    )(x1, *consts)


def _attn_kernel(q_ref, kt_ref, vt_ref, o_ref, v2_ref, acc_ref, c_ref):
    seq = q_ref.shape[1]
    n_chunks = seq // ATT_KC
    n_blocks = seq // ATT_QB
    row = lax.broadcasted_iota(jnp.int32, (ATT_QB, ATT_KC), 0)
    col = lax.broadcasted_iota(jnp.int32, (ATT_QB, ATT_KC), 1)
    causal = col < row
    tri = jnp.where(row > col, 1.0, 0.0).astype(BF16)
    first_head = lax.broadcasted_iota(jnp.int32, (ATT_QB, LANES), 1) < SB_HEAD_DIM
    rows = lambda m: slice(m * ATT_QB, (m + 1) * ATT_QB)

    first_head_rows = lax.broadcasted_iota(jnp.int32, (LANES, ATT_KC), 0) < SB_HEAD_DIM
    for j in range(n_chunks):
        vc = vt_ref[0, 0, j]
        zero_v = jnp.zeros_like(vc)
        v2_ref[j, :, :ATT_KC] = jnp.where(first_head_rows, vc, zero_v)
        v2_ref[j, :, ATT_KC:] = jnp.where(first_head_rows, zero_v, vc)

    def sweep(units, c, acc):
        n = len(units)
        vals = [dict() for _ in range(n)]
        q2 = {}

        def scores(i):
            m, j = units[i]
            if m not in q2:
                q = q_ref[0, rows(m), :]
                zero = jnp.zeros_like(q)
                q2[m] = jnp.concatenate(
                    [jnp.where(first_head, q, zero), jnp.where(first_head, zero, q)], axis=0)
            vals[i]["z2"] = jnp.dot(q2[m], kt_ref[0, 0, j], preferred_element_type=F32)

        def softplus(i):
            m, j = units[i]
            z2 = vals[i].pop("z2")
            log_sig, sp_bf16, row_sum = [], [], []
            for h in range(HEADS_PER_SLAB):
                z = z2[h * ATT_QB:(h + 1) * ATT_QB]
                if j == m:
                    z = jnp.where(causal, z, MASKED_LOGIT)
                sp = jnp.maximum(
                    jnp.log(1.0 + jnp.exp2(jnp.minimum(z, SOFTPLUS_CLAMP))) * LOG2E, z)
                log_sig.append(z - sp)
                sp_bf16.append(sp.astype(BF16))
                row_sum.append(jnp.sum(sp, axis=1, keepdims=True))
            vals[i].update(log_sig=log_sig, sp_bf16=sp_bf16, row_sum=row_sum)

        def later_sums(i):
            vals[i]["later"] = [jnp.dot(s, tri, preferred_element_type=F32)
                                for s in vals[i].pop("sp_bf16")]

        def weights(i):
            m, j = units[i]
            v = vals[i]
            ws = []
            for h in range(HEADS_PER_SLAB):
                ws.append(jnp.exp2(v["log_sig"][h] - v["later"][h] - c[m][h]).astype(BF16))
                c[m][h] = c[m][h] + v["row_sum"][h]
            vals[i] = {"w": jnp.concatenate(ws, axis=1)}

        def weighted_values(i):
            m, j = units[i]
            d = lax.dot_general(v2_ref[j], vals[i].pop("w"), (((1,), (1,)), ((), ())),
                                preferred_element_type=F32)
            acc[m] = d if acc.get(m) is None else acc[m] + d

        for i in range(n + 2):
            if i < n:
                scores(i)
            if 0 <= i - 1 < n:
                later_sums(i - 1)
            if 0 <= i - 2 < n:
                weighted_values(i - 2)
            if i < n:
                softplus(i)
            if 0 <= i - 1 < n:
                weights(i - 1)

    blocks = range(n_blocks - 1, -1, -1)
    c = {m: [jnp.zeros((ATT_QB, 1), F32) for _ in range(HEADS_PER_SLAB)] for m in blocks}
    acc = {}
    sweep([(m, j) for m in blocks for j in (m, m - 1) if j >= 0], c, acc)
    for m in blocks:
        acc_ref[:, rows(m)] = acc[m]
    unfinished = [m for m in blocks if m >= 2]
    if unfinished:
        c_min = None
        for m in unfinished:
            for h in range(HEADS_PER_SLAB):
                c_ref[h, rows(m), :] = c[m][h]
                c_min = c[m][h] if c_min is None else jnp.minimum(c_min, c[m][h])

        @pl.when(jnp.min(c_min) < ATT_SKIP_LOG2)
        def _():
            c2 = {m: [c_ref[h, rows(m), :] for h in range(HEADS_PER_SLAB)] for m in unfinished}
            acc2 = {m: acc_ref[:, rows(m)] for m in unfinished}
            sweep([(m, j) for m in unfinished for j in range(m - 2, -1, -1)], c2, acc2)
            for m in unfinished:
                acc_ref[:, rows(m)] = acc2[m]

    o_ref[0] = acc_ref[...].T.astype(BF16)


def _attn(q, kt, vt):
    bsz, seq, d = q.shape
    n_slab = d // LANES
    slab = lambda b, p: (b, 0, p)
    tiles = pl.BlockSpec((1, 1, seq // ATT_KC, LANES, ATT_KC), lambda b, p: (b, p, 0, 0, 0))
    return pl.pallas_call(
        _attn_kernel,
        out_shape=jax.ShapeDtypeStruct((bsz, seq, d), BF16),
        grid=(bsz, n_slab),
        in_specs=[pl.BlockSpec((1, seq, LANES), slab), tiles, tiles],
        out_specs=pl.BlockSpec((1, seq, LANES), slab),
        scratch_shapes=[pltpu.VMEM((seq // ATT_KC, LANES, HEADS_PER_SLAB * ATT_KC), BF16),
                        pltpu.VMEM((LANES, seq), F32),
                        pltpu.VMEM((HEADS_PER_SLAB, seq, 1), F32)],
        compiler_params=pltpu.CompilerParams(
            dimension_semantics=("arbitrary", "arbitrary"), vmem_limit_bytes=32 * MIB),
        name="attn",
    )(q, kt, vt)


def _merge_kernel(x_ref, ya_ref, yb_ref, g_ref, wa_ref, wb_ref, wo_ref, o_ref):
    d = x_ref.shape[1]
    ma = jnp.dot(ya_ref[...], wa_ref[...], preferred_element_type=F32)
    mb = jnp.dot(yb_ref[...], wb_ref[...], preferred_element_type=F32)
    merged = (g_ref[:, :d] * ma + g_ref[:, d:] * mb).astype(BF16)
    o_ref[...] = x_ref[...] + jnp.dot(merged, wo_ref[...], preferred_element_type=F32)


def _merge(x1, ya, yb, gates, wa, wb, wo, tm):
    t, d = x1.shape
    row = lambda i: (i, 0)
    consts = [wa, wb, wo]
    return pl.pallas_call(
        _merge_kernel,
        out_shape=jax.ShapeDtypeStruct((t, d), F32),
        grid=(t // tm,),
        in_specs=[pl.BlockSpec((tm, d), row), pl.BlockSpec((tm, d), row),
                  pl.BlockSpec((tm, d), row), pl.BlockSpec((tm, 2 * d), row)]
                 + [_const_spec(a.shape) for a in consts],
        out_specs=pl.BlockSpec((tm, d), row),
        compiler_params=pltpu.CompilerParams(
            dimension_semantics=("arbitrary",), vmem_limit_bytes=52 * MIB),
        name="merge",
    )(x1, ya, yb, gates, *consts)


def _ff2_kernel(x_ref, n_ref, wg_ref, wu_ref, wd_ref, nf_ref, o_ref):
    x = x_ref[...]
    h = _rms(x, n_ref[...]).astype(BF16)
    o_ref[...] = _rms(x + 0.5 * _swiglu(h, wg_ref, wu_ref, wd_ref), nf_ref[...])


def _ff2(x2d, norm, wg, wu, wd, nf, tm):
    t, d = x2d.shape
    consts = [norm, wg, wu, wd, nf]
    return pl.pallas_call(
        _ff2_kernel,
        out_shape=jax.ShapeDtypeStruct((t, d), F32),
        grid=(t // tm,),
        in_specs=[pl.BlockSpec((tm, d), lambda i: (i, 0))] + [_const_spec(a.shape) for a in consts],
        out_specs=pl.BlockSpec((tm, d), lambda i: (i, 0)),
        compiler_params=pltpu.CompilerParams(
            dimension_semantics=("arbitrary",), vmem_limit_bytes=52 * MIB),
        name="ff2",
    )(x2d, *consts)


def _row_tile(seq, want):
    tm = min(want, seq)
    assert seq % tm == 0 and tm % ATT_KC == 0
    return tm


def kernel(x, ff1_norm, ff1_w_gate, ff1_w_up, ff1_w_down, mix_norm, w_in, b_gate, gmlp_ln_g, gmlp_ln_b, gmlp_w_s, gmlp_b_s, w_branch_a, w_branch_b, w_out, ff2_norm, ff2_w_gate, ff2_w_up, ff2_w_down, final_norm):
    bsz, seq, d = x.shape
    assert ff1_norm.shape[0] == 1, "the final RMSNorm is fused into the single layer's last kernel"
    assert seq % ATT_QB == 0 and d == SB_HEADS * SB_HEAD_DIM and ATT_QB == ATT_KC
    gm = d
    x2d = x.reshape(bsz * seq, d)
    assert ff1_w_gate.shape[2] % FF_CHUNK == 0
    x1 = _ff1(x2d, ff1_norm, ff1_w_gate[0].astype(BF16), ff1_w_up[0].astype(BF16),
              ff1_w_down[0].astype(BF16), _row_tile(seq, 512))

    w = w_in[0]
    o_q, o_k, o_v, o_g = 2 * gm, 2 * gm + d, 2 * gm + 2 * d, 2 * gm + 3 * d
    ya, q, kt, vt, gates = _proj(
        x1, mix_norm,
        w[:, :gm].astype(BF16), w[:, gm:o_q].astype(BF16), w[:, o_q:o_k].astype(BF16),
        w[:, o_k:o_v].T.astype(BF16), w[:, o_v:o_g].T.astype(BF16), w[:, o_g:].astype(BF16),
        b_gate, gmlp_ln_g.reshape(1, gm), gmlp_ln_b.reshape(1, gm),
        gmlp_w_s[0], gmlp_b_s[0][:, :, None], bsz, seq, _row_tile(seq, 512))

    yb = _attn(q.reshape(bsz, seq, d), kt, vt).reshape(bsz * seq, d)

    x2 = _merge(x1, ya, yb, gates, w_branch_a[0].astype(BF16), w_branch_b[0].astype(BF16),
                w_out[0].astype(BF16), _row_tile(seq, 512))
    y = _ff2(x2, ff2_norm, ff2_w_gate[0].astype(BF16), ff2_w_up[0].astype(BF16),
             ff2_w_down[0].astype(BF16), final_norm[None], _row_tile(seq, 512))
    return y.reshape(bsz, seq, d)
```
